```python
import math
import jax, jax.numpy as jnp
from jax import lax
import numpy as np

D_MODEL = 1024
BATCH = 2
SEQ = 8192
DEPTH = 2

N_BRANCHES = 4
BRANCH_WIDTH = D_MODEL // 2
D_FF = 2816
ROPE_THETA = 10000.0
NORM_EPS = 1e-6

M2_HEAD_DIM = 64
M2_HEADS = BRANCH_WIDTH // M2_HEAD_DIM
M2_GROUPS = 2
M2_STATE = 64
M2_CONV = 4
M2_CHUNK = 128
M2_CONV_DIM = BRANCH_WIDTH + 2 * M2_GROUPS * M2_STATE
M2_IN = BRANCH_WIDTH + M2_CONV_DIM + M2_HEADS

S5_GROUP = 16
S5_GROUPS = BRANCH_WIDTH // S5_GROUP
S5_STATE = 64
S5_IN = BRANCH_WIDTH

DSA_HEAD_DIM = 64
DSA_HEADS = BRANCH_WIDTH // DSA_HEAD_DIM
DSA_IDX_HEADS = 4
DSA_IDX_DIM = 64
DSA_TOPK = 256
DSA_QBLOCK = 128
DSA_IN = 3 * BRANCH_WIDTH + DSA_IDX_HEADS * DSA_IDX_DIM + DSA_IDX_DIM + DSA_IDX_HEADS

HG_HEADS = 4
HG_KDIM = 128
HG_VDIM = BRANCH_WIDTH // HG_HEADS
HG_CHUNK = 32
HG_IN = 2 * HG_HEADS * HG_KDIM + 2 * BRANCH_WIDTH

D_IN = M2_IN + S5_IN + DSA_IN + HG_IN
IN_SPLITS = [M2_IN, M2_IN + S5_IN, M2_IN + S5_IN + DSA_IN]

kernel_name = "hybrid_gated_ssd_s5_dsa_hgrn2_macaron"


def rms_norm(x, gain):
    xf = x.astype(jnp.float32)
    xf = xf * lax.rsqrt(jnp.mean(xf * xf, axis=-1, keepdims=True) + NORM_EPS)
    return (xf * gain.astype(jnp.float32)).astype(x.dtype)


def swiglu(x, w1, w3, w2):
    return (jax.nn.silu(x @ w1) * (x @ w3)) @ w2


def rope(x, positions):
    half = x.shape[-1] // 2
    inv_freq = ROPE_THETA ** (-jnp.arange(half, dtype=jnp.float32) / half)
    ang = positions.astype(jnp.float32)[..., None] * inv_freq
    cos = jnp.cos(ang)[:, :, None, :]
    sin = jnp.sin(ang)[:, :, None, :]
    xf = x.astype(jnp.float32)
    x1, x2 = xf[..., :half], xf[..., half:]
    return jnp.concatenate([x1 * cos - x2 * sin, x2 * cos + x1 * sin], axis=-1).astype(x.dtype)


def ssd_chunked(xh, dt, a, bg, cg):
    b, L, H, P = xh.shape
    nc = L // M2_CHUNK
    rep = H // M2_GROUPS
    bh = jnp.repeat(bg, rep, axis=2).reshape(b, nc, M2_CHUNK, H, M2_STATE)
    ch = jnp.repeat(cg, rep, axis=2).reshape(b, nc, M2_CHUNK, H, M2_STATE)
    xdt = (xh * dt[..., None]).reshape(b, nc, M2_CHUNK, H, P)
    ad = (dt * a).reshape(b, nc, M2_CHUNK, H).transpose(0, 3, 1, 2)
    a_cs = jnp.cumsum(ad, axis=-1)
    causal = jnp.tril(jnp.ones((M2_CHUNK, M2_CHUNK), dtype=bool))
    seg = a_cs[..., :, None] - a_cs[..., None, :]
    decay_in = jnp.exp(jnp.where(causal, seg, -jnp.inf))
    scores = jnp.einsum("bcqhn,bcshn->bhcqs", ch, bh) * decay_in
    y_diag = jnp.einsum("bhcqs,bcshp->bcqhp", scores, xdt)
    decay_to_end = jnp.exp(a_cs[..., -1:] - a_cs)
    states = jnp.einsum("bcshn,bhcs,bcshp->bchpn", bh, decay_to_end, xdt)
    a_tot = jnp.pad(a_cs[..., -1], ((0, 0), (0, 0), (1, 0)))
    tot_cs = jnp.cumsum(a_tot, axis=-1)
    causal_c = jnp.tril(jnp.ones((nc + 1, nc + 1), dtype=bool))
    decay_chunk = jnp.exp(jnp.where(causal_c, tot_cs[..., :, None] - tot_cs[..., None, :], -jnp.inf))
    states = jnp.concatenate([jnp.zeros_like(states[:, :1]), states], axis=1)
    states_in = jnp.einsum("bhzc,bchpn->bzhpn", decay_chunk, states)[:, :-1]
    y_off = jnp.einsum("bcqhn,bchpn,bhcq->bcqhp", ch, states_in, jnp.exp(a_cs))
    return (y_diag + y_off).reshape(b, L, H, P)


def mamba2_branch(p_in, conv_w, conv_b, dt_bias, a_log, d_skip, norm_g):
    b, L, _ = p_in.shape
    z, xbc, dt_raw = jnp.split(p_in, [BRANCH_WIDTH, BRANCH_WIDTH + M2_CONV_DIM], axis=-1)
    xbc = lax.conv_general_dilated(xbc, conv_w[:, None, :], window_strides=(1,),
                                   padding=[(M2_CONV - 1, 0)],
                                   dimension_numbers=("NWC", "WIO", "NWC"),
                                   feature_group_count=M2_CONV_DIM)
    xbc = jax.nn.silu(xbc + conv_b)
    xs, bs, cs = jnp.split(xbc, [BRANCH_WIDTH, BRANCH_WIDTH + M2_GROUPS * M2_STATE], axis=-1)
    dt = jax.nn.softplus(dt_raw + dt_bias)
    a = -jnp.exp(a_log)
    xh = xs.reshape(b, L, M2_HEADS, M2_HEAD_DIM)
    y = ssd_chunked(xh, dt, a, bs.reshape(b, L, M2_GROUPS, M2_STATE), cs.reshape(b, L, M2_GROUPS, M2_STATE))
    y = (y + d_skip[:, None] * xh).reshape(b, L, BRANCH_WIDTH)
    return rms_norm(y * jax.nn.silu(z), norm_g)


def _complex_affine_combine(e1, e2):
    a1r, a1i, b1r, b1i = e1
    a2r, a2i, b2r, b2i = e2
    return (a2r * a1r - a2i * a1i, a2r * a1i + a2i * a1r,
            a2r * b1r - a2i * b1i + b2r, a2r * b1i + a2i * b1r + b2i)


def s5_branch(u, lam_re, lam_im, b_re, b_im, c_re, c_im, d_skip, log_dt, w_glu):
    b, L, _ = u.shape
    ug = u.reshape(b, L, S5_GROUPS, S5_GROUP)
    step = jnp.exp(log_dt)[:, None]
    ldt_re, ldt_im = lam_re * step, lam_im * step
    mag = jnp.exp(ldt_re)
    ab_re, ab_im = mag * jnp.cos(ldt_im), mag * jnp.sin(ldt_im)
    den = lam_re * lam_re + lam_im * lam_im
    q_re = ((ab_re - 1.0) * lam_re + ab_im * lam_im) / den
    q_im = (ab_im * lam_re - (ab_re - 1.0) * lam_im) / den
    bb_re = q_re[..., None] * b_re - q_im[..., None] * b_im
    bb_im = q_re[..., None] * b_im + q_im[..., None] * b_re
    bu_re = jnp.einsum("blgi,gpi->blgp", ug, bb_re)
    bu_im = jnp.einsum("blgi,gpi->blgp", ug, bb_im)
    a_re = jnp.broadcast_to(ab_re, (1, L) + ab_re.shape)
    a_im = jnp.broadcast_to(ab_im, (1, L) + ab_im.shape)
    _, _, s_re, s_im = lax.associative_scan(_complex_affine_combine, (a_re, a_im, bu_re, bu_im), axis=1)
    y = jnp.einsum("blgp,gip->blgi", s_re, c_re) - jnp.einsum("blgp,gip->blgi", s_im, c_im)
    y = y.reshape(b, L, BRANCH_WIDTH) + d_skip * u
    y = jax.nn.gelu(y)
    g_lin, g_gate = jnp.split(y @ w_glu, 2, axis=-1)
    return g_lin * jax.nn.sigmoid(g_gate)


def dsa_branch(p_in, positions, q_norm, k_norm):
    b, L, _ = p_in.shape
    bw = BRANCH_WIDTH
    idx_q = DSA_IDX_HEADS * DSA_IDX_DIM
    q, k, v, qi, ki, wi = jnp.split(p_in, [bw, 2 * bw, 3 * bw, 3 * bw + idx_q, 3 * bw + idx_q + DSA_IDX_DIM], axis=-1)
    q = rope(rms_norm(q.reshape(b, L, DSA_HEADS, DSA_HEAD_DIM), q_norm), positions)
    k = rope(rms_norm(k.reshape(b, L, DSA_HEADS, DSA_HEAD_DIM), k_norm), positions).reshape(b, L, bw)
    qi = qi.reshape(b, L, DSA_IDX_HEADS, DSA_IDX_DIM)
    wi = wi * (DSA_IDX_HEADS ** -0.5)
    topk = min(DSA_TOPK, L // 4)
    nb = L // DSA_QBLOCK
    key_pos = jnp.arange(L)
    att_scale = DSA_HEAD_DIM ** -0.5
    gather = jax.vmap(lambda arr, ix: arr[ix])

    def block(args):
        blk, qb, qib, wib = args
        t = blk * DSA_QBLOCK + jnp.arange(DSA_QBLOCK)
        rel = jax.nn.relu(jnp.einsum("bqhd,bsd->bqhs", qib, ki) * (DSA_IDX_DIM ** -0.5))
        score = jnp.einsum("bqh,bqhs->bqs", wib, rel).astype(jnp.float32)
        score = jnp.where((key_pos[None, :] <= t[:, None])[None], score, -jnp.inf)
        _, idx = lax.top_k(score, topk)
        valid = idx <= t[None, :, None]
        k_sel = gather(k, idx).reshape(b, DSA_QBLOCK, topk, DSA_HEADS, DSA_HEAD_DIM)
        v_sel = gather(v, idx).reshape(b, DSA_QBLOCK, topk, DSA_HEADS, DSA_HEAD_DIM)
        logits = jnp.einsum("bqhd,bqkhd->bhqk", qb, k_sel).astype(jnp.float32) * att_scale
        logits = jnp.where(valid[:, None], logits, -jnp.inf)
        prob = jax.nn.softmax(logits, axis=-1).astype(v.dtype)
        return jnp.einsum("bhqk,bqkhd->bqhd", prob, v_sel)

    def to_blocks(t):
        return t.reshape((b, nb, DSA_QBLOCK) + t.shape[2:]).swapaxes(0, 1)

    out = lax.map(block, (jnp.arange(nb), to_blocks(q), to_blocks(qi), to_blocks(wi)))
    return out.swapaxes(0, 1).reshape(b, L, bw)


def hgrn2_chunked(q, k, v, log_f):
    b, L, H, K = q.shape
    V = v.shape[-1]
    nc = L // HG_CHUNK
    dt = v.dtype

    def shp(t):
        return t.reshape((b, nc, HG_CHUNK) + t.shape[2:])

    q, k, v, log_f = shp(q), shp(k), shp(v), shp(log_f)
    cum = jnp.cumsum(log_f, axis=2)
    last = cum[:, :, -1:]
    q_dec = (q * jnp.exp(cum)).astype(dt)
    k_dec = (k * jnp.exp(-cum)).astype(dt)
    k_end = (k * jnp.exp(last - cum)).astype(dt)
    causal = jnp.tril(jnp.ones((HG_CHUNK, HG_CHUNK), dtype=bool))
    att = jnp.where(causal, jnp.einsum("bcqhk,bcshk->bchqs", q_dec, k_dec), 0.0).astype(dt)
    o_intra = jnp.einsum("bchqs,bcshv->bcqhv", att, v)
    chunk_state = jnp.einsum("bcshk,bcshv->bchkv", k_end, v)
    chunk_decay = jnp.exp(last[:, :, 0]).astype(dt)

    def step(state, inp):
        decay, upd = inp
        return decay[..., None] * state + upd, state

    init = jnp.zeros((b, H, K, V), dtype=dt)
    _, states_in = lax.scan(step, init, (chunk_decay.swapaxes(0, 1), chunk_state.swapaxes(0, 1)))
    states_in = states_in.swapaxes(0, 1)
    o_inter = jnp.einsum("bcqhk,bchkv->bcqhv", q_dec, states_in)
    return (o_intra + o_inter).reshape(b, L, H, V)


def hgrn2_branch(p_in, lower_bound, norm_g):
    b, L, _ = p_in.shape
    hk = HG_HEADS * HG_KDIM
    q, f, i, g = jnp.split(p_in, [hk, 2 * hk, 2 * hk + BRANCH_WIDTH], axis=-1)
    q = jax.nn.silu(q).reshape(b, L, HG_HEADS, HG_KDIM)
    f = f.reshape(b, L, HG_HEADS, HG_KDIM)
    lb = lower_bound.reshape(HG_HEADS, HG_KDIM).astype(p_in.dtype)
    forget = lb + (1.0 - lb) * jax.nn.sigmoid(f)
    log_f = jnp.log(forget.astype(jnp.float32))
    key = (1.0 - lb) * jax.nn.sigmoid(-f)
    v = i.reshape(b, L, HG_HEADS, HG_VDIM)
    o = hgrn2_chunked(q, key, v, log_f)
    o = rms_norm(o, norm_g.reshape(HG_HEADS, HG_VDIM)).reshape(b, L, BRANCH_WIDTH)
    return o * jax.nn.sigmoid(g)


def setup_inputs(seed: int = 0) -> dict:
    key = jax.random.key(seed)
    ks = iter(jax.random.split(key, 48))
    f32 = jnp.float32

    def nrm(shape, scale):
        return jax.random.normal(next(ks), shape, f32) * scale

    def gain(shape):
        return 1.0 + nrm(shape, 0.02)

    L = DEPTH
    x = jax.random.normal(next(ks), (BATCH, SEQ, D_MODEL), f32)
    positions = jnp.broadcast_to(jnp.arange(SEQ, dtype=jnp.int32), (BATCH, SEQ))
    ffn1_norm = gain((L, D_MODEL))
    ffn1_w1 = nrm((L, D_MODEL, D_FF), D_MODEL ** -0.5)
    ffn1_w3 = nrm((L, D_MODEL, D_FF), D_MODEL ** -0.5)
    ffn1_w2 = nrm((L, D_FF, D_MODEL), D_FF ** -0.5)
    mix_norm = gain((L, D_MODEL))
    w_in = nrm((L, D_MODEL, D_IN), D_MODEL ** -0.5)
    w_gate = nrm((L, D_MODEL, N_BRANCHES * D_MODEL), D_MODEL ** -0.5)
    w_branch = nrm((L, N_BRANCHES, BRANCH_WIDTH, D_MODEL), BRANCH_WIDTH ** -0.5)
    w_out = nrm((L, D_MODEL, D_MODEL), D_MODEL ** -0.5)
    m2_conv_w = nrm((L, M2_CONV, M2_CONV_DIM), M2_CONV ** -0.5)
    m2_conv_b = nrm((L, M2_CONV_DIM), 0.01)
    dt0 = jnp.exp(jax.random.uniform(next(ks), (L, M2_HEADS), f32, math.log(1e-3), math.log(1e-1)))
    m2_dt_bias = dt0 + jnp.log(-jnp.expm1(-dt0))
    m2_a_log = jnp.log(jax.random.uniform(next(ks), (L, M2_HEADS), f32, 1.0, 16.0))
    m2_d = 1.0 + nrm((L, M2_HEADS), 0.1)
    m2_norm = gain((L, BRANCH_WIDTH))
    s5_lam_re = -0.5 + nrm((L, S5_GROUPS, S5_STATE), 0.01)
    s5_lam_im = math.pi * jnp.arange(S5_STATE, dtype=f32) + nrm((L, S5_GROUPS, S5_STATE), 0.01)
    s5_b_re = nrm((L, S5_GROUPS, S5_STATE, S5_GROUP), (2 * S5_GROUP) ** -0.5)
    s5_b_im = nrm((L, S5_GROUPS, S5_STATE, S5_GROUP), (2 * S5_GROUP) ** -0.5)
    s5_c_re = nrm((L, S5_GROUPS, S5_GROUP, S5_STATE), (2 * S5_STATE) ** -0.5)
    s5_c_im = nrm((L, S5_GROUPS, S5_GROUP, S5_STATE), (2 * S5_STATE) ** -0.5)
    s5_d = nrm((L, BRANCH_WIDTH), 1.0)
    s5_log_dt = jax.random.uniform(next(ks), (L, S5_GROUPS), f32, math.log(1e-3), math.log(1e-1))
    s5_w_glu = nrm((L, BRANCH_WIDTH, 2 * BRANCH_WIDTH), BRANCH_WIDTH ** -0.5)
    dsa_q_norm = gain((L, DSA_HEAD_DIM))
    dsa_k_norm = gain((L, DSA_HEAD_DIM))
    hg_gamma = nrm((L, HG_HEADS * HG_KDIM), 0.1)
    hg_norm = gain((L, BRANCH_WIDTH))
    ffn2_norm = gain((L, D_MODEL))
    ffn2_w1 = nrm((L, D_MODEL, D_FF), D_MODEL ** -0.5)
    ffn2_w3 = nrm((L, D_MODEL, D_FF), D_MODEL ** -0.5)
    ffn2_w2 = nrm((L, D_FF, D_MODEL), D_FF ** -0.5)
    return {"x": x, "positions": positions,
            "ffn1_norm": ffn1_norm, "ffn1_w1": ffn1_w1, "ffn1_w3": ffn1_w3, "ffn1_w2": ffn1_w2,
            "mix_norm": mix_norm, "w_in": w_in, "w_gate": w_gate, "w_branch": w_branch, "w_out": w_out,
            "m2_conv_w": m2_conv_w, "m2_conv_b": m2_conv_b, "m2_dt_bias": m2_dt_bias,
            "m2_a_log": m2_a_log, "m2_d": m2_d, "m2_norm": m2_norm,
            "s5_lam_re": s5_lam_re, "s5_lam_im": s5_lam_im, "s5_b_re": s5_b_re, "s5_b_im": s5_b_im,
            "s5_c_re": s5_c_re, "s5_c_im": s5_c_im, "s5_d": s5_d, "s5_log_dt": s5_log_dt,
            "s5_w_glu": s5_w_glu, "dsa_q_norm": dsa_q_norm, "dsa_k_norm": dsa_k_norm,
            "hg_gamma": hg_gamma, "hg_norm": hg_norm,
            "ffn2_norm": ffn2_norm, "ffn2_w1": ffn2_w1, "ffn2_w3": ffn2_w3, "ffn2_w2": ffn2_w2}


def reference(x, positions, ffn1_norm, ffn1_w1, ffn1_w3, ffn1_w2, mix_norm, w_in, w_gate, w_branch, w_out,
              m2_conv_w, m2_conv_b, m2_dt_bias, m2_a_log, m2_d, m2_norm,
              s5_lam_re, s5_lam_im, s5_b_re, s5_b_im, s5_c_re, s5_c_im, s5_d, s5_log_dt, s5_w_glu,
              dsa_q_norm, dsa_k_norm, hg_gamma, hg_norm, ffn2_norm, ffn2_w1, ffn2_w3, ffn2_w2):
    b, L, _ = x.shape
    lbs = jnp.cumsum(jax.nn.softmax(hg_gamma.astype(jnp.float32), axis=0), axis=0)
    lbs = lbs - lbs[0]
    for l in range(DEPTH):
        x = x + 0.5 * swiglu(rms_norm(x, ffn1_norm[l]), ffn1_w1[l], ffn1_w3[l], ffn1_w2[l])
        h = rms_norm(x, mix_norm[l])
        p = h @ w_in[l]
        p_m2, p_s5, p_dsa, p_hg = jnp.split(p, IN_SPLITS, axis=-1)
        y_a = mamba2_branch(p_m2, m2_conv_w[l], m2_conv_b[l], m2_dt_bias[l], m2_a_log[l], m2_d[l], m2_norm[l])
        y_b = s5_branch(p_s5, s5_lam_re[l], s5_lam_im[l], s5_b_re[l], s5_b_im[l], s5_c_re[l], s5_c_im[l],
                        s5_d[l], s5_log_dt[l], s5_w_glu[l])
        y_c = dsa_branch(p_dsa, positions, dsa_q_norm[l], dsa_k_norm[l])
        y_d = hgrn2_branch(p_hg, lbs[l], hg_norm[l])
        ys = jnp.stack([y_a, y_b, y_c, y_d], axis=2)
        proj = jnp.einsum("blgc,gcd->blgd", ys, w_branch[l])
        gates = jax.nn.sigmoid(h @ w_gate[l]).reshape(b, L, N_BRANCHES, D_MODEL)
        merged = jnp.einsum("blgd,blgd->bld", gates, proj)
        x = x + merged @ w_out[l]
        x = x + 0.5 * swiglu(rms_norm(x, ffn2_norm[l]), ffn2_w1[l], ffn2_w3[l], ffn2_w2[l])
    return x
```

```python
import functools
import math

import jax
import jax.numpy as jnp
from jax import lax
from jax.experimental import pallas as pl
from jax.experimental.pallas import tpu as pltpu

F32 = jnp.float32
BF16 = jnp.bfloat16
I32 = jnp.int32

D_MODEL = 1024
N_BRANCHES = 4
BW = D_MODEL // 2
D_FF = 2816
ROPE_THETA = 10000.0
NORM_EPS = 1e-6

M2_HEAD_DIM = 64
M2_HEADS = BW // M2_HEAD_DIM
M2_GROUPS = 2
M2_STATE = 64
M2_CONV = 4
M2_CHUNK = 128

S5_GROUP = 16
S5_GROUPS = BW // S5_GROUP
S5_STATE = 64
S5_CHUNK = 16
S5_GB = 8

DSA_HEAD_DIM = 64
DSA_HEADS = BW // DSA_HEAD_DIM
DSA_IDX_HEADS = 4
DSA_IDX_DIM = 64
DSA_TOPK = 256

HG_HEADS = 4
HG_KDIM = 128
HG_VDIM = BW // HG_HEADS
HG_CHUNK = 32

C_HGQ, C_HGF, C_HGI, C_HGG = 0, 512, 1024, 1536
C_DQ, C_DK, C_DV = 2048, 2560, 3072
C_MZ, C_MX = 3584, 4096
C_S5 = 4608
C_DQI = 5120
C_MBC = 5376
C_DKW = 5632
C_MDT = 5760
NP = 5888

LANE = 128
INT_MIN = -(2 ** 31)
NEG_BIG = -1e30
MASKED_SCORE = -3e38
DSA_TILE = 256
DSA_VPAD = 16
DSA_VROWS = DSA_HEADS * (DSA_HEAD_DIM + DSA_VPAD)
VMEM_LIMIT = 56 * 1024 * 1024


def _cparams(sem):
    return pltpu.CompilerParams(dimension_semantics=sem, vmem_limit_bytes=VMEM_LIMIT)


def _const_spec(shape):
    nd = len(shape)
    return pl.BlockSpec(shape, lambda *_: (0,) * nd, pipeline_mode=pl.Buffered(1))


def _silu(x):
    return x * jax.nn.sigmoid(x)


def _ffn_body(x_ref, g_ref, w1_ref, w3_ref, w2_ref, o_ref, *, ff_chunk):
    x = x_ref[...]
    n = x * lax.rsqrt(jnp.mean(x * x, axis=-1, keepdims=True) + NORM_EPS) * g_ref[...]
    n = n.astype(BF16)
    acc = jnp.zeros(x.shape, F32)
    for c in range(D_FF // ff_chunk):
        sl = slice(c * ff_chunk, (c + 1) * ff_chunk)
        h1 = jnp.dot(n, w1_ref[:, sl], preferred_element_type=F32)
        h3 = jnp.dot(n, w3_ref[:, sl], preferred_element_type=F32)
        g = (_silu(h1) * h3).astype(BF16)
        acc = acc + jnp.dot(g, w2_ref[sl, :], preferred_element_type=F32)
    o_ref[...] = x + 0.5 * acc


def _ffn(x, gain, w1, w3, w2, tm=512, ff_chunk=256):
    t = x.shape[0]
    return pl.pallas_call(
        functools.partial(_ffn_body, ff_chunk=ff_chunk),
        grid=(t // tm,),
        in_specs=[
            pl.BlockSpec((tm, D_MODEL), lambda i: (i, 0)),
            _const_spec((1, D_MODEL)),
            _const_spec((D_MODEL, D_FF)),
            _const_spec((D_MODEL, D_FF)),
            _const_spec((D_FF, D_MODEL)),
        ],
        out_specs=pl.BlockSpec((tm, D_MODEL), lambda i: (i, 0)),
        out_shape=jax.ShapeDtypeStruct((t, D_MODEL), F32),
        compiler_params=_cparams(("parallel",)),
        name="ffn",
    )(x, gain.reshape(1, D_MODEL), w1, w3, w2)


def _proj_body(x_ref, g_ref, w_ref, p_ref, h_ref):
    x = x_ref[...]
    h = x * lax.rsqrt(jnp.mean(x * x, axis=-1, keepdims=True) + NORM_EPS) * g_ref[...]
    hb = h.astype(BF16)
    h_ref[...] = hb
    p_ref[...] = jnp.dot(hb, w_ref[...], preferred_element_type=F32)


def _proj(x, gain, w_pad, tm=256):
    t = x.shape[0]
    return pl.pallas_call(
        _proj_body,
        grid=(t // tm,),
        in_specs=[
            pl.BlockSpec((tm, D_MODEL), lambda i: (i, 0)),
            _const_spec((1, D_MODEL)),
            _const_spec((D_MODEL, NP)),
        ],
        out_specs=[
            pl.BlockSpec((tm, NP), lambda i: (i, 0)),
            pl.BlockSpec((tm, D_MODEL), lambda i: (i, 0)),
        ],
        out_shape=[
            jax.ShapeDtypeStruct((t, NP), F32),
            jax.ShapeDtypeStruct((t, D_MODEL), BF16),
        ],
        compiler_params=_cparams(("parallel",)),
        name="proj",
    )(x, gain.reshape(1, D_MODEL), w_pad)


def _mamba_body(z_ref, x_ref, bc_ref, dt_ref, cwx_ref, cbx_ref, cwbc_ref, cbbc_ref, dtb_ref, a_ref,
                dvec_ref, ng_ref, o_ref, xbuf, bcbuf, st_ref, ybuf, *, tt):
    q = M2_CHUNK
    pad = 8

    @pl.when(pl.program_id(1) == 0)
    def _():
        xbuf[0:pad, :] = jnp.zeros((pad, BW), F32)
        bcbuf[0:pad, :] = jnp.zeros((pad, 2 * LANE), F32)
        st_ref[...] = jnp.zeros(st_ref.shape, F32)

    xbuf[pad:pad + tt, :] = x_ref[...]
    bcbuf[pad:pad + tt, :] = bc_ref[...]

    def conv(buf, cw_ref, cb_ref):
        acc = cb_ref[...]
        for w in range(M2_CONV):
            off = pad - (M2_CONV - 1) + w
            acc = acc + cw_ref[w:w + 1, :] * buf[off:off + tt, :]
        return _silu(acc)

    xs = conv(xbuf, cwx_ref, cbx_ref)
    bc = conv(bcbuf, cwbc_ref, cbbc_ref)
    xbuf[0:pad, :] = xbuf[tt:tt + pad, :]
    bcbuf[0:pad, :] = bcbuf[tt:tt + pad, :]

    dtr = dt_ref[...] + dtb_ref[...]
    dt = jnp.maximum(dtr, 0.0) + jnp.log(1.0 + jnp.exp(-jnp.abs(dtr)))
    ad = dt * a_ref[...]

    rq = lax.broadcasted_iota(I32, (q, q), 0)
    cq = lax.broadcasted_iota(I32, (q, q), 1)
    causal = rq >= cq
    tril = causal.astype(F32)

    for c in range(tt // q):
        rows = slice(c * q, (c + 1) * q)
        a_cs = jnp.dot(tril, ad[rows], precision=lax.Precision.HIGHEST, preferred_element_type=F32)
        a_cs_t = a_cs.T
        xs_c = xs[rows]
        dt_c = dt[rows]
        for g in range(M2_GROUPS):
            b_g = bc[rows, g * M2_STATE:(g + 1) * M2_STATE]
            c_g = bc[rows, LANE + g * M2_STATE:LANE + (g + 1) * M2_STATE]
            b_gt = b_g.T
            c_gb = c_g.astype(BF16)
            gmat = jnp.dot(c_gb, b_gt.astype(BF16), preferred_element_type=F32)
            for hh in range(M2_HEADS // M2_GROUPS):
                h = g * (M2_HEADS // M2_GROUPS) + hh
                col = a_cs[:, h:h + 1]
                row = a_cs_t[h:h + 1, :]
                a_last = a_cs[q - 1:q, h:h + 1]
                decay = jnp.exp(jnp.where(causal, col - row, -jnp.inf))
                xdt = (xs_c[:, h * M2_HEAD_DIM:(h + 1) * M2_HEAD_DIM] * dt_c[:, h:h + 1]).astype(BF16)
                y_diag = jnp.dot((gmat * decay).astype(BF16), xdt, preferred_element_type=F32)
                st_in = st_ref[h]
                y_off = jnp.dot(c_gb, st_in.astype(BF16), preferred_element_type=F32) * jnp.exp(col)
                bw_t = (b_gt * jnp.exp(a_last - row)).astype(BF16)
                st_ref[h] = jnp.exp(a_last) * st_in + jnp.dot(bw_t, xdt, preferred_element_type=F32)
                ybuf[rows, h * M2_HEAD_DIM:(h + 1) * M2_HEAD_DIM] = y_diag + y_off

    y = ybuf[...] + dvec_ref[...] * xs
    yz = y * _silu(z_ref[...])
    o_ref[...] = (yz * lax.rsqrt(jnp.mean(yz * yz, axis=-1, keepdims=True) + NORM_EPS) * ng_ref[...]).astype(BF16)


def _mamba(p3, conv_w, conv_b, dt_bias, a_log, d_skip, norm_g, tt=256):
    b, l, _ = p3.shape
    cwx, cwbc = conv_w[:, :BW], conv_w[:, BW:]
    cbx, cbbc = conv_b[:BW].reshape(1, BW), conv_b[BW:].reshape(1, 2 * LANE)
    dtb = jnp.zeros((1, LANE), F32).at[0, :M2_HEADS].set(dt_bias)
    a = jnp.zeros((1, LANE), F32).at[0, :M2_HEADS].set(-jnp.exp(a_log))
    dvec = jnp.repeat(d_skip, M2_HEAD_DIM).reshape(1, BW)

    def col(width, start):
        return pl.BlockSpec((None, tt, width), lambda bi, ti: (bi, ti, start // width))

    return pl.pallas_call(
        functools.partial(_mamba_body, tt=tt),
        grid=(b, l // tt),
        in_specs=[
            col(BW, C_MZ), col(BW, C_MX), col(2 * LANE, C_MBC), col(LANE, C_MDT),
            _const_spec((M2_CONV, BW)), _const_spec((1, BW)),
            _const_spec((M2_CONV, 2 * LANE)), _const_spec((1, 2 * LANE)),
            _const_spec((1, LANE)), _const_spec((1, LANE)),
            _const_spec((1, BW)), _const_spec((1, BW)),
        ],
        out_specs=pl.BlockSpec((None, tt, BW), lambda bi, ti: (bi, ti, 0)),
        out_shape=jax.ShapeDtypeStruct((b, l, BW), BF16),
        scratch_shapes=[
            pltpu.VMEM((tt + 8, BW), F32),
            pltpu.VMEM((tt + 8, 2 * LANE), F32),
            pltpu.VMEM((M2_HEADS, M2_STATE, M2_HEAD_DIM), F32),
            pltpu.VMEM((tt, BW), F32),
        ],
        compiler_params=_cparams(("parallel", "arbitrary")),
        name="mamba",
    )(p3, p3, p3, p3, cwx, cbx, cwbc, cbbc, dtb, a, dvec, norm_g.reshape(1, BW))


def _s5_tables(lam_re, lam_im, b_re, b_im, c_re, c_im, log_dt):
    ck, gs, ps = S5_CHUNK, S5_GROUP, S5_STATE
    step = jnp.exp(log_dt)[:, None]
    ldt_re, ldt_im = lam_re * step, lam_im * step
    mag = jnp.exp(ldt_re)
    ab_re, ab_im = mag * jnp.cos(ldt_im), mag * jnp.sin(ldt_im)
    den = lam_re * lam_re + lam_im * lam_im
    q_re = ((ab_re - 1.0) * lam_re + ab_im * lam_im) / den
    q_im = (ab_im * lam_re - (ab_re - 1.0) * lam_im) / den
    bb_re = q_re[..., None] * b_re - q_im[..., None] * b_im
    bb_im = q_re[..., None] * b_im + q_im[..., None] * b_re
    n = jnp.arange(ck + 1, dtype=F32)[:, None, None]
    pm = jnp.exp(n * ldt_re[None])
    pw_re, pw_im = pm * jnp.cos(n * ldt_im[None]), pm * jnp.sin(n * ldt_im[None])
    ab_r = pw_re[..., None] * bb_re[None] - pw_im[..., None] * bb_im[None]
    ab_i = pw_re[..., None] * bb_im[None] + pw_im[..., None] * bb_re[None]
    kk = jnp.einsum("gip,ngpj->ngij", c_re, ab_r) - jnp.einsum("gip,ngpj->ngij", c_im, ab_i)
    tq = jnp.arange(ck)
    lag = tq[None, :] - tq[:, None]
    kl = jnp.where((lag >= 0)[:, :, None, None, None], kk[jnp.clip(lag, 0, ck)], 0.0)
    m = kl.transpose(2, 0, 4, 1, 3).reshape(S5_GROUPS, ck * gs, ck * gs)
    rev = ck - 1 - tq
    w_re = ab_r[rev].transpose(1, 0, 3, 2).reshape(S5_GROUPS, ck * gs, ps)
    w_im = ab_i[rev].transpose(1, 0, 3, 2).reshape(S5_GROUPS, ck * gs, ps)
    ca_r = c_re[None] * pw_re[1:, :, None, :] - c_im[None] * pw_im[1:, :, None, :]
    ca_i = c_re[None] * pw_im[1:, :, None, :] + c_im[None] * pw_re[1:, :, None, :]
    v_re = ca_r.transpose(1, 3, 0, 2).reshape(S5_GROUPS, ps, ck * gs)
    v_im = (-ca_i).transpose(1, 3, 0, 2).reshape(S5_GROUPS, ps, ck * gs)
    a16_re = pw_re[ck].reshape(S5_GROUPS // S5_GB, 1, S5_GB * ps)
    a16_im = pw_im[ck].reshape(S5_GROUPS // S5_GB, 1, S5_GB * ps)
    return (m.astype(BF16), w_re.astype(BF16), w_im.astype(BF16), v_re.astype(BF16), v_im.astype(BF16),
            a16_re, a16_im)


def _s5_perm():
    ck, gb, gs = S5_CHUNK, S5_GB, S5_GROUP
    r = jnp.arange(ck * gb * gs)
    dest = ((r % (gb * gs)) // gs) * (ck * gs) + (r // (gb * gs)) * gs + r % gs
    return (dest[:, None] == r[None, :]).astype(BF16)


def _s5_body(u_ref, perm_ref, m_ref, wre_ref, wim_ref, vre_ref, vim_ref, are_ref, aim_ref, y_ref,
             xre, xim, sre, sim, ybuf, *, lc):
    p, rw = S5_STATE, S5_CHUNK * S5_GROUP
    ug = jnp.dot(u_ref[...], perm_ref[...], preferred_element_type=F32).astype(BF16)
    for g in range(S5_GB):
        ugg = ug[:, g * rw:(g + 1) * rw]
        xre[:, g * p:(g + 1) * p] = jnp.dot(ugg, wre_ref[g], preferred_element_type=F32)
        xim[:, g * p:(g + 1) * p] = jnp.dot(ugg, wim_ref[g], preferred_element_type=F32)
    a_re = are_ref[...]
    a_im = aim_ref[...]

    def step(c, carry):
        s_re, s_im = carry
        sre[pl.ds(c, 1), :] = s_re
        sim[pl.ds(c, 1), :] = s_im
        n_re = a_re * s_re - a_im * s_im + xre[pl.ds(c, 1), :]
        n_im = a_re * s_im + a_im * s_re + xim[pl.ds(c, 1), :]
        return n_re, n_im

    zero = jnp.zeros((1, S5_GB * p), F32)
    lax.fori_loop(0, lc, step, (zero, zero))
    for g in range(S5_GB):
        y = jnp.dot(ug[:, g * rw:(g + 1) * rw], m_ref[g], preferred_element_type=F32)
        y = y + jnp.dot(sre[:, g * p:(g + 1) * p].astype(BF16), vre_ref[g], preferred_element_type=F32)
        y = y + jnp.dot(sim[:, g * p:(g + 1) * p].astype(BF16), vim_ref[g], preferred_element_type=F32)
        ybuf[:, g * rw:(g + 1) * rw] = y.astype(BF16)
    y_ref[...] = lax.dot_general(ybuf[...], perm_ref[...], (((1,), (1,)), ((), ())),
                                 preferred_element_type=F32).astype(BF16)


def _s5_core(u_rows, perm, tables):
    b, nb, lc, rw = u_rows.shape
    m, w_re, w_im, v_re, v_im, a_re, a_im = tables
    gb, sw, gw = S5_GB, S5_GB * S5_STATE, S5_CHUNK * S5_GROUP

    def grp(shape):
        return pl.BlockSpec((gb,) + shape, lambda bi, gi: (gi, 0, 0))

    def blk(shape):
        return pl.BlockSpec((None,) + shape, lambda bi, gi: (gi, 0, 0))

    return pl.pallas_call(
        functools.partial(_s5_body, lc=lc),
        grid=(b, nb),
        in_specs=[
            pl.BlockSpec((None, None, lc, rw), lambda bi, gi: (bi, gi, 0, 0)),
            _const_spec((rw, rw)),
            grp((gw, gw)), grp((gw, S5_STATE)), grp((gw, S5_STATE)), grp((S5_STATE, gw)), grp((S5_STATE, gw)),
            blk((1, sw)), blk((1, sw)),
        ],
        out_specs=pl.BlockSpec((None, None, lc, rw), lambda bi, gi: (bi, gi, 0, 0)),
        out_shape=jax.ShapeDtypeStruct((b, nb, lc, rw), BF16),
        scratch_shapes=[pltpu.VMEM((lc, sw), F32) for _ in range(4)] + [pltpu.VMEM((lc, rw), BF16)],
        compiler_params=_cparams(("parallel", "parallel")),
        name="s5_core",
    )(u_rows, perm, m, w_re, w_im, v_re, v_im, a_re, a_im)


def _s5_post_body(y_ref, u_ref, d_ref, w_ref, o_ref):
    y = y_ref[...].astype(F32) + d_ref[...] * u_ref[...]
    c = math.sqrt(2.0 / math.pi)
    ge = 0.5 * y * (1.0 + jnp.tanh(c * (y + 0.044715 * (y * y * y))))
    g2 = jnp.dot(ge.astype(BF16), w_ref[...], preferred_element_type=F32)
    o_ref[...] = (g2[:, :BW] * jax.nn.sigmoid(g2[:, BW:])).astype(BF16)


def _s5_post(y_ssm, p, d_skip, w_glu, tm=512):
    t = y_ssm.shape[0]
    return pl.pallas_call(
        _s5_post_body,
        grid=(t // tm,),
        in_specs=[
            pl.BlockSpec((tm, BW), lambda i: (i, 0)),
            pl.BlockSpec((tm, BW), lambda i: (i, C_S5 // BW)),
            _const_spec((1, BW)),
            _const_spec((BW, 2 * BW)),
        ],
        out_specs=pl.BlockSpec((tm, BW), lambda i: (i, 0)),
        out_shape=jax.ShapeDtypeStruct((t, BW), BF16),
        compiler_params=_cparams(("parallel",)),
        name="s5_post",
    )(y_ssm, p, d_skip.reshape(1, BW), w_glu)


def _s5(p, b, l, tables, d_skip, w_glu):
    lc = l // S5_CHUNK
    nb, bw = S5_GROUPS // S5_GB, S5_GB * S5_GROUP
    u = p[:, C_S5:C_S5 + BW].astype(BF16)
    u_rows = u.reshape(b, lc, S5_CHUNK, nb, bw).transpose(0, 3, 1, 2, 4).reshape(b, nb, lc, S5_CHUNK * bw)
    y_rows = _s5_core(u_rows, _s5_perm(), tables)
    y_ssm = y_rows.reshape(b, nb, lc, S5_CHUNK, bw).transpose(0, 2, 3, 1, 4).reshape(b * l, BW)
    return _s5_post(y_ssm, p, d_skip, w_glu)


def _dsa_prep_body(q_ref, k_ref, v_ref, qi_ref, kw_ref, pos_ref, qn_ref, kn_ref, freq_ref, sgn_ref, ones_ref,
                   qt_ref, ko_ref, vt_ref, kio_ref, qit_ref, wt_ref):
    ang = pos_ref[...].astype(F32) * freq_ref[...]
    reps = BW // LANE
    cos = jnp.concatenate([jnp.cos(ang)] * reps, axis=1)
    sin = jnp.concatenate([jnp.sin(ang) * sgn_ref[...]] * reps, axis=1)
    lane = lax.broadcasted_iota(I32, cos.shape, 1)
    first_half = (lane % DSA_HEAD_DIM) < (DSA_HEAD_DIM // 2)

    def norm_rope(x, gain, scale):
        x2 = x * x
        hi = x2.astype(BF16)
        lo = (x2 - hi.astype(F32)).astype(BF16)
        ms = (jnp.dot(hi, ones_ref[...], preferred_element_type=F32)
              + jnp.dot(lo, ones_ref[...], preferred_element_type=F32))
        xn = x * lax.rsqrt(ms * (1.0 / DSA_HEAD_DIM) + NORM_EPS) * gain
        half = DSA_HEAD_DIM // 2
        partner = jnp.where(first_half, pltpu.roll(xn, BW - half, 1), pltpu.roll(xn, half, 1))
        return (xn * cos + partner * sin) * scale

    qt_ref[...] = norm_rope(q_ref[...], qn_ref[...], DSA_HEAD_DIM ** -0.5 * math.log2(math.e)).T.astype(BF16)
    ko_ref[...] = norm_rope(k_ref[...], kn_ref[...], 1.0).astype(BF16)
    vt = v_ref[...].T
    ones = jnp.ones((DSA_VPAD, vt.shape[1]), F32)
    pieces = []
    for h in range(DSA_HEADS):
        pieces += [vt[h * DSA_HEAD_DIM:(h + 1) * DSA_HEAD_DIM, :], ones]
    vt_ref[...] = jnp.concatenate(pieces, axis=0).astype(BF16)
    kw = kw_ref[...]
    kio_ref[...] = kw.astype(BF16)
    qit_ref[...] = qi_ref[...].T.astype(BF16)
    wt = kw.T[DSA_IDX_DIM:DSA_IDX_DIM + 8, :]
    wt_ref[...] = wt * (DSA_IDX_HEADS ** -0.5 * DSA_IDX_DIM ** -0.5)


def _dsa_prep(p3, pos, q_norm, k_norm):
    b, l, _ = p3.shape
    tt = DSA_TILE
    half = DSA_HEAD_DIM // 2
    inv_freq = ROPE_THETA ** (-jnp.arange(half, dtype=F32) / half)
    freq = jnp.tile(jnp.concatenate([inv_freq, inv_freq]), LANE // DSA_HEAD_DIM).reshape(1, LANE)
    sgn = jnp.tile(jnp.concatenate([-jnp.ones(half, F32), jnp.ones(half, F32)]), LANE // DSA_HEAD_DIM).reshape(1, LANE)
    hid = jnp.arange(BW) // DSA_HEAD_DIM
    ones_bd = (hid[:, None] == hid[None, :]).astype(BF16)
    qn = jnp.tile(q_norm, DSA_HEADS).reshape(1, BW)
    kn = jnp.tile(k_norm, DSA_HEADS).reshape(1, BW)

    def col(width, start):
        return pl.BlockSpec((None, tt, width), lambda bi, ti: (bi, ti, start // width))

    def rows_t(n):
        return pl.BlockSpec((None, n, tt), lambda bi, ti: (bi, 0, ti))

    return pl.pallas_call(
        _dsa_prep_body,
        grid=(b, l // tt),
        in_specs=[
            col(BW, C_DQ), col(BW, C_DK), col(BW, C_DV), col(2 * LANE, C_DQI), col(LANE, C_DKW),
            pl.BlockSpec((None, tt, 1), lambda bi, ti: (bi, ti, 0)),
            _const_spec((1, BW)), _const_spec((1, BW)), _const_spec((1, LANE)), _const_spec((1, LANE)),
            _const_spec((BW, BW)),
        ],
        out_specs=[
            rows_t(BW),
            pl.BlockSpec((None, tt, BW), lambda bi, ti: (bi, ti, 0)),
            pl.BlockSpec((None, None, DSA_VROWS, tt), lambda bi, ti: (bi, ti, 0, 0)),
            pl.BlockSpec((None, tt, LANE), lambda bi, ti: (bi, ti, 0)),
            rows_t(2 * LANE),
            rows_t(8),
        ],
        out_shape=[
            jax.ShapeDtypeStruct((b, BW, l), BF16),
            jax.ShapeDtypeStruct((b, l, BW), BF16),
            jax.ShapeDtypeStruct((b, l // tt, DSA_VROWS, tt), BF16),
            jax.ShapeDtypeStruct((b, l, LANE), BF16),
            jax.ShapeDtypeStruct((b, 2 * LANE, l), BF16),
            jax.ShapeDtypeStruct((b, 8, l), F32),
        ],
        compiler_params=_cparams(("parallel", "parallel")),
        name="dsa_prep",
    )(p3, p3, p3, p3, p3, pos.reshape(b, l, 1), qn, kn, freq, sgn, ones_bd)


def _fold8(w, op):
    parts = [w[8 * k:8 * (k + 1), :] for k in range(w.shape[0] // 8)]
    while len(parts) > 1:
        parts = [op(parts[2 * k], parts[2 * k + 1]) for k in range(len(parts) // 2)]
    return parts[0]


def _pattern_to_float(u):
    bits = jnp.where(u < 0, u ^ INT_MIN, ~u)
    return lax.bitcast_convert_type(bits, F32)


def _dsa_body(qt_ref, qit_ref, wt_ref, k_ref, vt_ref, ki_ref, o_ref, sc_ref, m_ref, acc_ref):
    t = DSA_TILE
    d = DSA_HEAD_DIM
    i = pl.program_id(1)
    key_l = lax.broadcasted_iota(I32, (t, t), 0)
    qry_g = i * t + lax.broadcasted_iota(I32, (t, t), 1)

    def index_tile(jt, masked):
        ks = pl.ds(pl.multiple_of(jt * t, t), t)
        kib = ki_ref[ks, 0:DSA_IDX_DIM]
        sc = jnp.zeros((t, t), F32)
        for h in range(DSA_IDX_HEADS):
            dd = jnp.dot(kib, qit_ref[h * DSA_IDX_DIM:(h + 1) * DSA_IDX_DIM, :], preferred_element_type=F32)
            sc = sc + jnp.maximum(dd, 0.0) * wt_ref[h:h + 1, :]
        if masked:
            sc = jnp.where(jt * t + key_l <= qry_g, sc, MASKED_SCORE)
        sc_ref[ks, :] = sc

    def index_loop(jt, carry):
        index_tile(jt, False)
        return carry

    lax.fori_loop(0, i, index_loop, 0)
    index_tile(i, True)

    n_tiles = i + 1

    def sweep(fn, init):
        acc = lax.fori_loop(0, n_tiles // 2, lambda j, a: fn(pl.multiple_of(j * 2 * t, 2 * t), 2 * t, a), init)
        return lax.cond(n_tiles % 2 == 1, lambda a: fn(pl.multiple_of((n_tiles - 1) * t, t), t, a), lambda a: a, acc)

    def count(pred):
        def body(r0, n, cnt):
            x = sc_ref[pl.ds(r0, n), :]
            key_g = r0 + lax.broadcasted_iota(I32, (n, t), 0)
            return cnt + _fold8(jnp.where(pred(x, key_g), 1.0, 0.0), jnp.add)
        return jnp.sum(sweep(body, jnp.zeros((8, t), F32)), axis=0, keepdims=True)

    def value_pass(b, v):
        cand = v | jnp.left_shift(jnp.int32(1), 31 - b)
        mid = _pattern_to_float(cand)
        tot = count(lambda x, key_g: x >= mid)
        return jnp.where(tot >= DSA_TOPK, cand, v)

    v = lax.fori_loop(0, 32, value_pass, jnp.zeros((1, t), I32))
    floor = _pattern_to_float(v)

    def next_score(pred):
        def body(r0, n, acc):
            x = sc_ref[pl.ds(r0, n), :]
            return jnp.minimum(acc, _fold8(jnp.where(pred(x), x, jnp.inf), jnp.minimum))
        return jnp.min(sweep(body, jnp.full((8, t), jnp.inf, F32)), axis=0, keepdims=True)

    thr = next_score(lambda x: x >= floor)
    n_gt = count(lambda x, key_g: x > thr)

    def too_low(carry):
        return jnp.max(jnp.where(carry[1] >= DSA_TOPK, 1, 0)) > 0

    def raise_thr(carry):
        th, ng = carry
        th = jnp.where(ng >= DSA_TOPK, next_score(lambda x: x > th), th)
        return th, count(lambda x, key_g: x > th)

    thr, n_gt = lax.while_loop(too_low, raise_thr, (thr, n_gt))
    n_ge = count(lambda x, key_g: x >= thr)
    need = DSA_TOPK - n_gt
    trim = n_ge > DSA_TOPK

    def index_pass(b, jc):
        cand = jc | jnp.left_shift(jnp.int32(1), 13 - b)
        tot = count(lambda x, key_g: (x == thr) & (key_g < cand))
        return jnp.where(tot < need, cand, jc)

    any_trim = jnp.max(jnp.where(trim, 1, 0)) > 0
    jc = lax.cond(any_trim,
                  lambda: lax.fori_loop(0, 14, index_pass, jnp.zeros((1, t), I32)),
                  lambda: jnp.zeros((1, t), I32))
    jcut = jnp.where(trim, jc, jnp.int32(2 ** 30))

    dv = d + DSA_VPAD
    m_ref[...] = jnp.full(m_ref.shape, NEG_BIG, F32)
    acc_ref[...] = jnp.zeros(acc_ref.shape, F32)

    def attend_tile(jt, masked):
        ks = pl.ds(pl.multiple_of(jt * t, t), t)
        x = sc_ref[ks, :]
        key_g = jt * t + key_l
        sel = (x > thr) | ((x == thr) & (key_g <= jcut))
        if masked:
            sel = sel & (key_g <= qry_g)
        bias = jnp.where(sel, 0.0, NEG_BIG)
        lgs = [jnp.dot(k_ref[ks, h * d:(h + 1) * d], qt_ref[h * d:(h + 1) * d, :], preferred_element_type=F32) + bias
               for h in range(DSA_HEADS)]
        for h in range(DSA_HEADS):
            hv = slice(h * dv, (h + 1) * dv)
            m_old = m_ref[h:h + 1, :]
            m_new = jnp.maximum(m_old, jnp.max(_fold8(lgs[h], jnp.maximum), axis=0, keepdims=True))
            alpha = jnp.exp2(m_old - m_new)
            pexp = jnp.exp2(lgs[h] - m_new).astype(BF16)
            m_ref[h:h + 1, :] = m_new
            acc_ref[hv, :] = alpha * acc_ref[hv, :] + jnp.dot(vt_ref[jt, hv, :], pexp, preferred_element_type=F32)

    def attend_loop(jt, carry):
        attend_tile(jt, False)
        return carry

    lax.fori_loop(0, i, attend_loop, 0)
    attend_tile(i, True)
    outs = [acc_ref[h * dv:h * dv + d, :] / acc_ref[h * dv + d:h * dv + d + 1, :] for h in range(DSA_HEADS)]
    o_ref[...] = jnp.concatenate(outs, axis=0).T.astype(BF16)


def _dsa_main(q_t, qi_t, w_t, k_r, v_t, ki_b):
    b, l, _ = k_r.shape
    t = DSA_TILE

    def rows_t(n):
        return pl.BlockSpec((None, n, t), lambda bi, qi: (bi, 0, qi))

    def whole(shape):
        nd = len(shape)
        return pl.BlockSpec((None,) + shape, lambda bi, qi: (bi,) + (0,) * nd, pipeline_mode=pl.Buffered(1))

    return pl.pallas_call(
        _dsa_body,
        grid=(b, l // t),
        in_specs=[rows_t(BW), rows_t(2 * LANE), rows_t(8),
                  whole((l, BW)), whole((l // t, DSA_VROWS, t)), whole((l, LANE))],
        out_specs=pl.BlockSpec((None, t, BW), lambda bi, qi: (bi, qi, 0)),
        out_shape=jax.ShapeDtypeStruct((b, l, BW), BF16),
        scratch_shapes=[
            pltpu.VMEM((l, t), F32),
            pltpu.VMEM((DSA_HEADS, t), F32),
            pltpu.VMEM((DSA_VROWS, t), F32),
        ],
        compiler_params=_cparams(("parallel", "arbitrary")),
        name="dsa_main",
    )(q_t, qi_t, w_t, k_r, v_t, ki_b)


def _hgrn_body(q_ref, f_ref, i_ref, g_ref, lb_ref, ng_ref, o_ref, st_ref, obuf, *, tt):
    ck = HG_CHUNK

    @pl.when(pl.program_id(1) == 0)
    def _():
        st_ref[...] = jnp.zeros(st_ref.shape, F32)

    lb = lb_ref[...]
    f = f_ref[...]
    qq = _silu(q_ref[...])
    log_f = jnp.log(lb + (1.0 - lb) * jax.nn.sigmoid(f))
    key = (1.0 - lb) * jax.nn.sigmoid(-f)
    v = i_ref[...]

    r = lax.broadcasted_iota(I32, (tt, tt), 0)
    c = lax.broadcasted_iota(I32, (tt, tt), 1)
    blk_causal = ((r // ck) == (c // ck)) & (r >= c)
    cum = jnp.dot(blk_causal.astype(F32), log_f, precision=lax.Precision.HIGHEST, preferred_element_type=F32)
    q_dec = qq * jnp.exp(cum)
    k_dec = key * jnp.exp(-cum)
    q_dec_b = q_dec.astype(BF16)
    k_dec_b = k_dec.astype(BF16)
    v_b = v.astype(BF16)

    for h in range(HG_HEADS):
        hs = slice(h * HG_KDIM, (h + 1) * HG_KDIM)
        att = lax.dot_general(q_dec_b[:, hs], k_dec_b[:, hs], (((1,), (1,)), ((), ())), preferred_element_type=F32)
        att = jnp.where(blk_causal, att, 0.0).astype(BF16)
        obuf[:, hs] = jnp.dot(att, v_b[:, hs], preferred_element_type=F32)

    for cc in range(tt // ck):
        rows = slice(cc * ck, (cc + 1) * ck)
        last = cum[cc * ck + ck - 1:cc * ck + ck, :]
        k_end = (key[rows] * jnp.exp(last - cum[rows])).astype(BF16)
        dec = jnp.exp(last)
        for h in range(HG_HEADS):
            hs = slice(h * HG_KDIM, (h + 1) * HG_KDIM)
            st = st_ref[h]
            o_inter = lax.dot_general(q_dec_b[rows, hs], st.astype(BF16), (((1,), (1,)), ((), ())),
                                      preferred_element_type=F32)
            obuf[rows, hs] = obuf[rows, hs] + o_inter
            upd = lax.dot_general(v_b[rows, hs], k_end[:, hs], (((0,), (0,)), ((), ())),
                                  preferred_element_type=F32)
            st_ref[h] = st * dec[:, hs] + upd

    gate = jax.nn.sigmoid(g_ref[...])
    for h in range(HG_HEADS):
        hs = slice(h * HG_VDIM, (h + 1) * HG_VDIM)
        o = obuf[:, hs]
        on = o * lax.rsqrt(jnp.mean(o * o, axis=-1, keepdims=True) + NORM_EPS) * ng_ref[:, hs]
        o_ref[:, hs] = (on * gate[:, hs]).astype(BF16)


def _hgrn(p3, lb, norm_g, tt=256):
    b, l, _ = p3.shape

    def col(start):
        return pl.BlockSpec((None, tt, BW), lambda bi, ti: (bi, ti, start // BW))

    return pl.pallas_call(
        functools.partial(_hgrn_body, tt=tt),
        grid=(b, l // tt),
        in_specs=[col(C_HGQ), col(C_HGF), col(C_HGI), col(C_HGG), _const_spec((1, BW)), _const_spec((1, BW))],
        out_specs=pl.BlockSpec((None, tt, BW), lambda bi, ti: (bi, ti, 0)),
        out_shape=jax.ShapeDtypeStruct((b, l, BW), BF16),
        scratch_shapes=[pltpu.VMEM((HG_HEADS, HG_VDIM, HG_KDIM), F32), pltpu.VMEM((tt, BW), F32)],
        compiler_params=_cparams(("parallel", "arbitrary")),
        name="hgrn",
    )(p3, p3, p3, p3, lb.reshape(1, BW), norm_g.reshape(1, BW))


def _merge_body(x_ref, h_ref, ya_ref, yb_ref, yc_ref, yd_ref, wg_ref, wb_ref, wo_ref, o_ref):
    h = h_ref[...]
    merged = jnp.zeros(x_ref.shape, F32)
    for g, y_ref in enumerate((ya_ref, yb_ref, yc_ref, yd_ref)):
        gate = jax.nn.sigmoid(jnp.dot(h, wg_ref[:, g * D_MODEL:(g + 1) * D_MODEL], preferred_element_type=F32))
        merged = merged + gate * jnp.dot(y_ref[...], wb_ref[g], preferred_element_type=F32)
    o_ref[...] = x_ref[...] + jnp.dot(merged.astype(BF16), wo_ref[...], preferred_element_type=F32)


def _merge(x, h, ys, w_gate, w_branch, w_out, tm=512):
    t = x.shape[0]
    row = lambda w: pl.BlockSpec((tm, w), lambda i: (i, 0))
    return pl.pallas_call(
        _merge_body,
        grid=(t // tm,),
        in_specs=[row(D_MODEL), row(D_MODEL), row(BW), row(BW), row(BW), row(BW),
                  _const_spec((D_MODEL, N_BRANCHES * D_MODEL)),
                  _const_spec((N_BRANCHES, BW, D_MODEL)),
                  _const_spec((D_MODEL, D_MODEL))],
        out_specs=row(D_MODEL),
        out_shape=jax.ShapeDtypeStruct((t, D_MODEL), F32),
        compiler_params=_cparams(("parallel",)),
        name="merge",
    )(x, h, *ys, w_gate, w_branch, w_out)


def _pad_w_in(w_in):
    o = 0
    seg = {}
    for name, width in (("mz", BW), ("mx", BW), ("mbc", 2 * LANE), ("mdt", M2_HEADS), ("s5", BW),
                        ("dq", BW), ("dk", BW), ("dv", BW), ("dqi", 2 * LANE), ("dki", DSA_IDX_DIM),
                        ("dwi", DSA_IDX_HEADS), ("hq", BW), ("hf", BW), ("hi", BW), ("hg", BW)):
        seg[name] = w_in[:, o:o + width]
        o += width
    z = lambda n: jnp.zeros((D_MODEL, n), w_in.dtype)
    cols = [seg["hq"], seg["hf"], seg["hi"], seg["hg"], seg["dq"], seg["dk"], seg["dv"], seg["mz"], seg["mx"],
            seg["s5"], seg["dqi"], seg["mbc"],
            seg["dki"], seg["dwi"], z(LANE - DSA_IDX_DIM - DSA_IDX_HEADS),
            seg["mdt"], z(LANE - M2_HEADS)]
    return jnp.concatenate(cols, axis=1)


def kernel(x, positions, ffn1_norm, ffn1_w1, ffn1_w3, ffn1_w2, mix_norm, w_in, w_gate, w_branch, w_out, m2_conv_w, m2_conv_b, m2_dt_bias, m2_a_log, m2_d, m2_norm, s5_lam_re, s5_lam_im, s5_b_re, s5_b_im, s5_c_re, s5_c_im, s5_d, s5_log_dt, s5_w_glu, dsa_q_norm, dsa_k_norm, hg_gamma, hg_norm, ffn2_norm, ffn2_w1, ffn2_w3, ffn2_w2):
    b, l, _ = x.shape
    t = b * l
    depth = w_in.shape[0]
    lbs = jnp.cumsum(jax.nn.softmax(hg_gamma.astype(F32), axis=0), axis=0)
    lbs = lbs - lbs[0]
    xf = x.reshape(t, D_MODEL)
    w_pad = jax.vmap(_pad_w_in)(w_in).astype(BF16)
    s5_tabs = jax.vmap(_s5_tables)(s5_lam_re, s5_lam_im, s5_b_re, s5_b_im, s5_c_re, s5_c_im, s5_log_dt)
    for li in range(depth):
        xf = _ffn(xf, ffn1_norm[li], ffn1_w1[li].astype(BF16), ffn1_w3[li].astype(BF16), ffn1_w2[li].astype(BF16))
        p, h = _proj(xf, mix_norm[li], w_pad[li])
        p3 = p.reshape(b, l, NP)
        y_a = _mamba(p3, m2_conv_w[li], m2_conv_b[li], m2_dt_bias[li], m2_a_log[li], m2_d[li], m2_norm[li])
        y_b = _s5(p, b, l, [tb[li] for tb in s5_tabs], s5_d[li], s5_w_glu[li].astype(BF16))
        q_t, k_r, v_t, ki_b, qi_t, w_t = _dsa_prep(p3, positions, dsa_q_norm[li], dsa_k_norm[li])
        y_c = _dsa_main(q_t, qi_t, w_t, k_r, v_t, ki_b)
        y_d = _hgrn(p3, lbs[li], hg_norm[li])
        ys = (y_a.reshape(t, BW), y_b, y_c.reshape(t, BW), y_d.reshape(t, BW))
        xf = _merge(xf, h, ys, w_gate[li].astype(BF16), w_branch[li].astype(BF16), w_out[li].astype(BF16))
        xf = _ffn(xf, ffn2_norm[li], ffn2_w1[li].astype(BF16), ffn2_w3[li].astype(BF16), ffn2_w2[li].astype(BF16))
    return xf.reshape(b, l, D_MODEL)
```

```python
import functools
import math

import jax
import jax.numpy as jnp
from jax import lax
from jax.experimental import pallas as pl
from jax.experimental.pallas import tpu as pltpu

F32 = jnp.float32
BF16 = jnp.bfloat16
I32 = jnp.int32

D_MODEL = 1024
N_BRANCHES = 4
BW = D_MODEL // 2
D_FF = 2816
ROPE_THETA = 10000.0
NORM_EPS = 1e-6

M2_HEAD_DIM = 64
M2_HEADS = BW // M2_HEAD_DIM
M2_GROUPS = 2
M2_STATE = 64
M2_CONV = 4
M2_CHUNK = 128

S5_GROUP = 16
S5_GROUPS = BW // S5_GROUP
S5_STATE = 64
S5_CHUNK = 16
S5_GB = 8

DSA_HEAD_DIM = 64
DSA_HEADS = BW // DSA_HEAD_DIM
DSA_IDX_HEADS = 4
DSA_IDX_DIM = 64
DSA_TOPK = 256

HG_HEADS = 4
HG_KDIM = 128
HG_VDIM = BW // HG_HEADS
HG_CHUNK = 32

C_HGQ, C_HGF, C_HGI, C_HGG = 0, 512, 1024, 1536
C_DQ, C_DK, C_DV = 2048, 2560, 3072
C_MZ, C_MX = 3584, 4096
C_S5 = 4608
C_DQI = 5120
C_MBC = 5376
C_DKW = 5632
C_MDT = 5760
NP = 5888

LANE = 128
INT_MIN = -(2 ** 31)
NEG_BIG = -1e30
MASKED_SCORE = -3e38
DSA_TILE = 256
DSA_QK_AHEAD = 8
DSA_VPAD = 16
DSA_VROWS = DSA_HEADS * (DSA_HEAD_DIM + DSA_VPAD)
VMEM_LIMIT = 56 * 1024 * 1024


def _cparams(sem):
    return pltpu.CompilerParams(dimension_semantics=sem, vmem_limit_bytes=VMEM_LIMIT)


def _const_spec(shape):
    nd = len(shape)
    return pl.BlockSpec(shape, lambda *_: (0,) * nd, pipeline_mode=pl.Buffered(1))


def _silu(x):
    return x * jax.nn.sigmoid(x)


def _ffn_body(x_ref, g_ref, w1_ref, w3_ref, w2_ref, o_ref, *, ff_chunk):
    x = x_ref[...]
    n = x * lax.rsqrt(jnp.mean(x * x, axis=-1, keepdims=True) + NORM_EPS) * g_ref[...]
    n = n.astype(BF16)
    acc = jnp.zeros(x.shape, F32)
    for c in range(D_FF // ff_chunk):
        sl = slice(c * ff_chunk, (c + 1) * ff_chunk)
        h1 = jnp.dot(n, w1_ref[:, sl], preferred_element_type=F32)
        h3 = jnp.dot(n, w3_ref[:, sl], preferred_element_type=F32)
        g = (_silu(h1) * h3).astype(BF16)
        acc = acc + jnp.dot(g, w2_ref[sl, :], preferred_element_type=F32)
    o_ref[...] = x + 0.5 * acc


def _ffn(x, gain, w1, w3, w2, tm=512, ff_chunk=256):
    t = x.shape[0]
    return pl.pallas_call(
        functools.partial(_ffn_body, ff_chunk=ff_chunk),
        grid=(t // tm,),
        in_specs=[
            pl.BlockSpec((tm, D_MODEL), lambda i: (i, 0)),
            _const_spec((1, D_MODEL)),
            _const_spec((D_MODEL, D_FF)),
            _const_spec((D_MODEL, D_FF)),
            _const_spec((D_FF, D_MODEL)),
        ],
        out_specs=pl.BlockSpec((tm, D_MODEL), lambda i: (i, 0)),
        out_shape=jax.ShapeDtypeStruct((t, D_MODEL), F32),
        compiler_params=_cparams(("parallel",)),
        name="ffn",
    )(x, gain.reshape(1, D_MODEL), w1, w3, w2)


def _proj_body(x_ref, g_ref, w_ref, p_ref, h_ref):
    x = x_ref[...]
    h = x * lax.rsqrt(jnp.mean(x * x, axis=-1, keepdims=True) + NORM_EPS) * g_ref[...]
    hb = h.astype(BF16)
    h_ref[...] = hb
    p_ref[...] = jnp.dot(hb, w_ref[...], preferred_element_type=F32)


def _proj(x, gain, w_pad, tm=256):
    t = x.shape[0]
    return pl.pallas_call(
        _proj_body,
        grid=(t // tm,),
        in_specs=[
            pl.BlockSpec((tm, D_MODEL), lambda i: (i, 0)),
            _const_spec((1, D_MODEL)),
            _const_spec((D_MODEL, NP)),
        ],
        out_specs=[
            pl.BlockSpec((tm, NP), lambda i: (i, 0)),
            pl.BlockSpec((tm, D_MODEL), lambda i: (i, 0)),
        ],
        out_shape=[
            jax.ShapeDtypeStruct((t, NP), F32),
            jax.ShapeDtypeStruct((t, D_MODEL), BF16),
        ],
        compiler_params=_cparams(("parallel",)),
        name="proj",
    )(x, gain.reshape(1, D_MODEL), w_pad)


def _mamba_body(z_ref, x_ref, bc_ref, dt_ref, cwx_ref, cbx_ref, cwbc_ref, cbbc_ref, dtb_ref, a_ref,
                dvec_ref, ng_ref, o_ref, xbuf, bcbuf, st_ref, ybuf, *, tt):
    q = M2_CHUNK
    pad = 8

    @pl.when(pl.program_id(1) == 0)
    def _():
        xbuf[0:pad, :] = jnp.zeros((pad, BW), F32)
        bcbuf[0:pad, :] = jnp.zeros((pad, 2 * LANE), F32)
        st_ref[...] = jnp.zeros(st_ref.shape, F32)

    xbuf[pad:pad + tt, :] = x_ref[...]
    bcbuf[pad:pad + tt, :] = bc_ref[...]

    def conv(buf, cw_ref, cb_ref):
        acc = cb_ref[...]
        for w in range(M2_CONV):
            off = pad - (M2_CONV - 1) + w
            acc = acc + cw_ref[w:w + 1, :] * buf[off:off + tt, :]
        return _silu(acc)

    xs = conv(xbuf, cwx_ref, cbx_ref)
    bc = conv(bcbuf, cwbc_ref, cbbc_ref)
    xbuf[0:pad, :] = xbuf[tt:tt + pad, :]
    bcbuf[0:pad, :] = bcbuf[tt:tt + pad, :]

    dtr = dt_ref[...] + dtb_ref[...]
    dt = jnp.maximum(dtr, 0.0) + jnp.log(1.0 + jnp.exp(-jnp.abs(dtr)))
    ad = dt * a_ref[...]

    rq = lax.broadcasted_iota(I32, (q, q), 0)
    cq = lax.broadcasted_iota(I32, (q, q), 1)
    causal = rq >= cq
    tril = causal.astype(F32)

    for c in range(tt // q):
        rows = slice(c * q, (c + 1) * q)
        a_cs = jnp.dot(tril, ad[rows], precision=lax.Precision.HIGHEST, preferred_element_type=F32)
        a_cs_t = a_cs.T
        xs_c = xs[rows]
        dt_c = dt[rows]
        for g in range(M2_GROUPS):
            b_g = bc[rows, g * M2_STATE:(g + 1) * M2_STATE]
            c_g = bc[rows, LANE + g * M2_STATE:LANE + (g + 1) * M2_STATE]
            b_gt = b_g.T
            c_gb = c_g.astype(BF16)
            gmat = jnp.dot(c_gb, b_gt.astype(BF16), preferred_element_type=F32)
            for hh in range(M2_HEADS // M2_GROUPS):
                h = g * (M2_HEADS // M2_GROUPS) + hh
                col = a_cs[:, h:h + 1]
                row = a_cs_t[h:h + 1, :]
                a_last = a_cs[q - 1:q, h:h + 1]
                decay = jnp.exp(jnp.where(causal, col - row, -jnp.inf))
                xdt = (xs_c[:, h * M2_HEAD_DIM:(h + 1) * M2_HEAD_DIM] * dt_c[:, h:h + 1]).astype(BF16)
                y_diag = jnp.dot((gmat * decay).astype(BF16), xdt, preferred_element_type=F32)
                st_in = st_ref[h]
                y_off = jnp.dot(c_gb, st_in.astype(BF16), preferred_element_type=F32) * jnp.exp(col)
                bw_t = (b_gt * jnp.exp(a_last - row)).astype(BF16)
                st_ref[h] = jnp.exp(a_last) * st_in + jnp.dot(bw_t, xdt, preferred_element_type=F32)
                ybuf[rows, h * M2_HEAD_DIM:(h + 1) * M2_HEAD_DIM] = y_diag + y_off

    y = ybuf[...] + dvec_ref[...] * xs
    yz = y * _silu(z_ref[...])
    o_ref[...] = (yz * lax.rsqrt(jnp.mean(yz * yz, axis=-1, keepdims=True) + NORM_EPS) * ng_ref[...]).astype(BF16)


def _mamba(p3, conv_w, conv_b, dt_bias, a_log, d_skip, norm_g, tt=256):
    b, l, _ = p3.shape
    cwx, cwbc = conv_w[:, :BW], conv_w[:, BW:]
    cbx, cbbc = conv_b[:BW].reshape(1, BW), conv_b[BW:].reshape(1, 2 * LANE)
    dtb = jnp.zeros((1, LANE), F32).at[0, :M2_HEADS].set(dt_bias)
    a = jnp.zeros((1, LANE), F32).at[0, :M2_HEADS].set(-jnp.exp(a_log))
    dvec = jnp.repeat(d_skip, M2_HEAD_DIM).reshape(1, BW)

    def col(width, start):
        return pl.BlockSpec((None, tt, width), lambda bi, ti: (bi, ti, start // width))

    return pl.pallas_call(
        functools.partial(_mamba_body, tt=tt),
        grid=(b, l // tt),
        in_specs=[
            col(BW, C_MZ), col(BW, C_MX), col(2 * LANE, C_MBC), col(LANE, C_MDT),
            _const_spec((M2_CONV, BW)), _const_spec((1, BW)),
            _const_spec((M2_CONV, 2 * LANE)), _const_spec((1, 2 * LANE)),
            _const_spec((1, LANE)), _const_spec((1, LANE)),
            _const_spec((1, BW)), _const_spec((1, BW)),
        ],
        out_specs=pl.BlockSpec((None, tt, BW), lambda bi, ti: (bi, ti, 0)),
        out_shape=jax.ShapeDtypeStruct((b, l, BW), BF16),
        scratch_shapes=[
            pltpu.VMEM((tt + 8, BW), F32),
            pltpu.VMEM((tt + 8, 2 * LANE), F32),
            pltpu.VMEM((M2_HEADS, M2_STATE, M2_HEAD_DIM), F32),
            pltpu.VMEM((tt, BW), F32),
        ],
        compiler_params=_cparams(("parallel", "arbitrary")),
        name="mamba",
    )(p3, p3, p3, p3, cwx, cbx, cwbc, cbbc, dtb, a, dvec, norm_g.reshape(1, BW))


def _s5_tables(lam_re, lam_im, b_re, b_im, c_re, c_im, log_dt):
    ck, gs, ps = S5_CHUNK, S5_GROUP, S5_STATE
    step = jnp.exp(log_dt)[:, None]
    ldt_re, ldt_im = lam_re * step, lam_im * step
    mag = jnp.exp(ldt_re)
    ab_re, ab_im = mag * jnp.cos(ldt_im), mag * jnp.sin(ldt_im)
    den = lam_re * lam_re + lam_im * lam_im
    q_re = ((ab_re - 1.0) * lam_re + ab_im * lam_im) / den
    q_im = (ab_im * lam_re - (ab_re - 1.0) * lam_im) / den
    bb_re = q_re[..., None] * b_re - q_im[..., None] * b_im
    bb_im = q_re[..., None] * b_im + q_im[..., None] * b_re
    n = jnp.arange(ck + 1, dtype=F32)[:, None, None]
    pm = jnp.exp(n * ldt_re[None])
    pw_re, pw_im = pm * jnp.cos(n * ldt_im[None]), pm * jnp.sin(n * ldt_im[None])
    ab_r = pw_re[..., None] * bb_re[None] - pw_im[..., None] * bb_im[None]
    ab_i = pw_re[..., None] * bb_im[None] + pw_im[..., None] * bb_re[None]
    kk = jnp.einsum("gip,ngpj->ngij", c_re, ab_r) - jnp.einsum("gip,ngpj->ngij", c_im, ab_i)
    tq = jnp.arange(ck)
    lag = tq[None, :] - tq[:, None]
    kl = jnp.where((lag >= 0)[:, :, None, None, None], kk[jnp.clip(lag, 0, ck)], 0.0)
    m = kl.transpose(2, 0, 4, 1, 3).reshape(S5_GROUPS, ck * gs, ck * gs)
    rev = ck - 1 - tq
    w_re = ab_r[rev].transpose(1, 0, 3, 2).reshape(S5_GROUPS, ck * gs, ps)
    w_im = ab_i[rev].transpose(1, 0, 3, 2).reshape(S5_GROUPS, ck * gs, ps)
    ca_r = c_re[None] * pw_re[1:, :, None, :] - c_im[None] * pw_im[1:, :, None, :]
    ca_i = c_re[None] * pw_im[1:, :, None, :] + c_im[None] * pw_re[1:, :, None, :]
    v_re = ca_r.transpose(1, 3, 0, 2).reshape(S5_GROUPS, ps, ck * gs)
    v_im = (-ca_i).transpose(1, 3, 0, 2).reshape(S5_GROUPS, ps, ck * gs)
    a16_re = pw_re[ck].reshape(S5_GROUPS // S5_GB, 1, S5_GB * ps)
    a16_im = pw_im[ck].reshape(S5_GROUPS // S5_GB, 1, S5_GB * ps)
    return (m.astype(BF16), w_re.astype(BF16), w_im.astype(BF16), v_re.astype(BF16), v_im.astype(BF16),
            a16_re, a16_im)


def _s5_perm():
    ck, gb, gs = S5_CHUNK, S5_GB, S5_GROUP
    r = jnp.arange(ck * gb * gs)
    dest = ((r % (gb * gs)) // gs) * (ck * gs) + (r // (gb * gs)) * gs + r % gs
    return (dest[:, None] == r[None, :]).astype(BF16)


def _s5_body(u_ref, perm_ref, m_ref, wre_ref, wim_ref, vre_ref, vim_ref, are_ref, aim_ref, y_ref,
             xre, xim, sre, sim, ybuf, *, lc):
    p, rw = S5_STATE, S5_CHUNK * S5_GROUP
    ug = jnp.dot(u_ref[...], perm_ref[...], preferred_element_type=F32).astype(BF16)
    for g in range(S5_GB):
        ugg = ug[:, g * rw:(g + 1) * rw]
        xre[:, g * p:(g + 1) * p] = jnp.dot(ugg, wre_ref[g], preferred_element_type=F32)
        xim[:, g * p:(g + 1) * p] = jnp.dot(ugg, wim_ref[g], preferred_element_type=F32)
    a_re = are_ref[...]
    a_im = aim_ref[...]

    def step(c, carry):
        s_re, s_im = carry
        sre[pl.ds(c, 1), :] = s_re
        sim[pl.ds(c, 1), :] = s_im
        n_re = a_re * s_re - a_im * s_im + xre[pl.ds(c, 1), :]
        n_im = a_re * s_im + a_im * s_re + xim[pl.ds(c, 1), :]
        return n_re, n_im

    zero = jnp.zeros((1, S5_GB * p), F32)
    lax.fori_loop(0, lc, step, (zero, zero))
    for g in range(S5_GB):
        y = jnp.dot(ug[:, g * rw:(g + 1) * rw], m_ref[g], preferred_element_type=F32)
        y = y + jnp.dot(sre[:, g * p:(g + 1) * p].astype(BF16), vre_ref[g], preferred_element_type=F32)
        y = y + jnp.dot(sim[:, g * p:(g + 1) * p].astype(BF16), vim_ref[g], preferred_element_type=F32)
        ybuf[:, g * rw:(g + 1) * rw] = y.astype(BF16)
    y_ref[...] = lax.dot_general(ybuf[...], perm_ref[...], (((1,), (1,)), ((), ())),
                                 preferred_element_type=F32).astype(BF16)


def _s5_core(u_rows, perm, tables):
    b, nb, lc, rw = u_rows.shape
    m, w_re, w_im, v_re, v_im, a_re, a_im = tables
    gb, sw, gw = S5_GB, S5_GB * S5_STATE, S5_CHUNK * S5_GROUP

    def grp(shape):
        return pl.BlockSpec((gb,) + shape, lambda bi, gi: (gi, 0, 0))

    def blk(shape):
        return pl.BlockSpec((None,) + shape, lambda bi, gi: (gi, 0, 0))

    return pl.pallas_call(
        functools.partial(_s5_body, lc=lc),
        grid=(b, nb),
        in_specs=[
            pl.BlockSpec((None, None, lc, rw), lambda bi, gi: (bi, gi, 0, 0)),
            _const_spec((rw, rw)),
            grp((gw, gw)), grp((gw, S5_STATE)), grp((gw, S5_STATE)), grp((S5_STATE, gw)), grp((S5_STATE, gw)),
            blk((1, sw)), blk((1, sw)),
        ],
        out_specs=pl.BlockSpec((None, None, lc, rw), lambda bi, gi: (bi, gi, 0, 0)),
        out_shape=jax.ShapeDtypeStruct((b, nb, lc, rw), BF16),
        scratch_shapes=[pltpu.VMEM((lc, sw), F32) for _ in range(4)] + [pltpu.VMEM((lc, rw), BF16)],
        compiler_params=_cparams(("parallel", "parallel")),
        name="s5_core",
    )(u_rows, perm, m, w_re, w_im, v_re, v_im, a_re, a_im)


def _s5_post_body(y_ref, u_ref, d_ref, w_ref, o_ref):
    y = y_ref[...].astype(F32) + d_ref[...] * u_ref[...]
    c = math.sqrt(2.0 / math.pi)
    ge = 0.5 * y * (1.0 + jnp.tanh(c * (y + 0.044715 * (y * y * y))))
    g2 = jnp.dot(ge.astype(BF16), w_ref[...], preferred_element_type=F32)
    o_ref[...] = (g2[:, :BW] * jax.nn.sigmoid(g2[:, BW:])).astype(BF16)


def _s5_post(y_ssm, p, d_skip, w_glu, tm=512):
    t = y_ssm.shape[0]
    return pl.pallas_call(
        _s5_post_body,
        grid=(t // tm,),
        in_specs=[
            pl.BlockSpec((tm, BW), lambda i: (i, 0)),
            pl.BlockSpec((tm, BW), lambda i: (i, C_S5 // BW)),
            _const_spec((1, BW)),
            _const_spec((BW, 2 * BW)),
        ],
        out_specs=pl.BlockSpec((tm, BW), lambda i: (i, 0)),
        out_shape=jax.ShapeDtypeStruct((t, BW), BF16),
        compiler_params=_cparams(("parallel",)),
        name="s5_post",
    )(y_ssm, p, d_skip.reshape(1, BW), w_glu)


def _s5(p, b, l, tables, d_skip, w_glu):
    lc = l // S5_CHUNK
    nb, bw = S5_GROUPS // S5_GB, S5_GB * S5_GROUP
    u = p[:, C_S5:C_S5 + BW].astype(BF16)
    u_rows = u.reshape(b, lc, S5_CHUNK, nb, bw).transpose(0, 3, 1, 2, 4).reshape(b, nb, lc, S5_CHUNK * bw)
    y_rows = _s5_core(u_rows, _s5_perm(), tables)
    y_ssm = y_rows.reshape(b, nb, lc, S5_CHUNK, bw).transpose(0, 2, 3, 1, 4).reshape(b * l, BW)
    return _s5_post(y_ssm, p, d_skip, w_glu)


def _dsa_prep_body(q_ref, k_ref, v_ref, qi_ref, kw_ref, pos_ref, qn_ref, kn_ref, freq_ref, sgn_ref, ones_ref,
                   qt_ref, ko_ref, vt_ref, kio_ref, qit_ref, wt_ref):
    ang = pos_ref[...].astype(F32) * freq_ref[...]
    reps = BW // LANE
    cos = jnp.concatenate([jnp.cos(ang)] * reps, axis=1)
    sin = jnp.concatenate([jnp.sin(ang) * sgn_ref[...]] * reps, axis=1)
    lane = lax.broadcasted_iota(I32, cos.shape, 1)
    first_half = (lane % DSA_HEAD_DIM) < (DSA_HEAD_DIM // 2)

    def norm_rope(x, gain, scale):
        x2 = x * x
        hi = x2.astype(BF16)
        lo = (x2 - hi.astype(F32)).astype(BF16)
        ms = (jnp.dot(hi, ones_ref[...], preferred_element_type=F32)
              + jnp.dot(lo, ones_ref[...], preferred_element_type=F32))
        xn = x * lax.rsqrt(ms * (1.0 / DSA_HEAD_DIM) + NORM_EPS) * gain
        half = DSA_HEAD_DIM // 2
        partner = jnp.where(first_half, pltpu.roll(xn, BW - half, 1), pltpu.roll(xn, half, 1))
        return (xn * cos + partner * sin) * scale

    qt_ref[...] = norm_rope(q_ref[...], qn_ref[...], DSA_HEAD_DIM ** -0.5 * math.log2(math.e)).T.astype(BF16)
    ko_ref[...] = norm_rope(k_ref[...], kn_ref[...], 1.0).astype(BF16)
    vt = v_ref[...].T
    ones = jnp.ones((DSA_VPAD, vt.shape[1]), F32)
    pieces = []
    for h in range(DSA_HEADS):
        pieces += [vt[h * DSA_HEAD_DIM:(h + 1) * DSA_HEAD_DIM, :], ones]
    vt_ref[...] = jnp.concatenate(pieces, axis=0).astype(BF16)
    kw = kw_ref[...]
    kio_ref[...] = kw.astype(BF16)
    qit_ref[...] = qi_ref[...].T.astype(BF16)
    wt = kw.T[DSA_IDX_DIM:DSA_IDX_DIM + 8, :]
    wt_ref[...] = wt * (DSA_IDX_HEADS ** -0.5 * DSA_IDX_DIM ** -0.5)


def _dsa_prep(p3, pos, q_norm, k_norm):
    b, l, _ = p3.shape
    tt = DSA_TILE
    half = DSA_HEAD_DIM // 2
    inv_freq = ROPE_THETA ** (-jnp.arange(half, dtype=F32) / half)
    freq = jnp.tile(jnp.concatenate([inv_freq, inv_freq]), LANE // DSA_HEAD_DIM).reshape(1, LANE)
    sgn = jnp.tile(jnp.concatenate([-jnp.ones(half, F32), jnp.ones(half, F32)]), LANE // DSA_HEAD_DIM).reshape(1, LANE)
    hid = jnp.arange(BW) // DSA_HEAD_DIM
    ones_bd = (hid[:, None] == hid[None, :]).astype(BF16)
    qn = jnp.tile(q_norm, DSA_HEADS).reshape(1, BW)
    kn = jnp.tile(k_norm, DSA_HEADS).reshape(1, BW)

    def col(width, start):
        return pl.BlockSpec((None, tt, width), lambda bi, ti: (bi, ti, start // width))

    def rows_t(n):
        return pl.BlockSpec((None, n, tt), lambda bi, ti: (bi, 0, ti))

    return pl.pallas_call(
        _dsa_prep_body,
        grid=(b, l // tt),
        in_specs=[
            col(BW, C_DQ), col(BW, C_DK), col(BW, C_DV), col(2 * LANE, C_DQI), col(LANE, C_DKW),
            pl.BlockSpec((None, tt, 1), lambda bi, ti: (bi, ti, 0)),
            _const_spec((1, BW)), _const_spec((1, BW)), _const_spec((1, LANE)), _const_spec((1, LANE)),
            _const_spec((BW, BW)),
        ],
        out_specs=[
            rows_t(BW),
            pl.BlockSpec((None, tt, BW), lambda bi, ti: (bi, ti, 0)),
            pl.BlockSpec((None, None, DSA_VROWS, tt), lambda bi, ti: (bi, ti, 0, 0)),
            pl.BlockSpec((None, tt, LANE), lambda bi, ti: (bi, ti, 0)),
            rows_t(2 * LANE),
            rows_t(8),
        ],
        out_shape=[
            jax.ShapeDtypeStruct((b, BW, l), BF16),
            jax.ShapeDtypeStruct((b, l, BW), BF16),
            jax.ShapeDtypeStruct((b, l // tt, DSA_VROWS, tt), BF16),
            jax.ShapeDtypeStruct((b, l, LANE), BF16),
            jax.ShapeDtypeStruct((b, 2 * LANE, l), BF16),
            jax.ShapeDtypeStruct((b, 8, l), F32),
        ],
        compiler_params=_cparams(("parallel", "parallel")),
        name="dsa_prep",
    )(p3, p3, p3, p3, p3, pos.reshape(b, l, 1), qn, kn, freq, sgn, ones_bd)


def _fold8(w, op):
    parts = [w[8 * k:8 * (k + 1), :] for k in range(w.shape[0] // 8)]
    while len(parts) > 1:
        parts = [op(parts[2 * k], parts[2 * k + 1]) for k in range(len(parts) // 2)]
    return parts[0]


def _pattern_to_float(u):
    bits = jnp.where(u < 0, u ^ INT_MIN, ~u)
    return lax.bitcast_convert_type(bits, F32)


def _dsa_body(qt_ref, qit_ref, wt_ref, k_ref, vt_ref, ki_ref, o_ref, sc_ref, m_ref, acc_ref):
    t = DSA_TILE
    d = DSA_HEAD_DIM
    i = pl.program_id(1)
    key_l = lax.broadcasted_iota(I32, (t, t), 0)
    qry_g = i * t + lax.broadcasted_iota(I32, (t, t), 1)

    def index_tile(jt, masked):
        ks = pl.ds(pl.multiple_of(jt * t, t), t)
        kib = ki_ref[ks, 0:DSA_IDX_DIM]
        sc = jnp.zeros((t, t), F32)
        for h in range(DSA_IDX_HEADS):
            dd = jnp.dot(kib, qit_ref[h * DSA_IDX_DIM:(h + 1) * DSA_IDX_DIM, :], preferred_element_type=F32)
            sc = sc + jnp.maximum(dd, 0.0) * wt_ref[h:h + 1, :]
        if masked:
            sc = jnp.where(jt * t + key_l <= qry_g, sc, MASKED_SCORE)
        sc_ref[ks, :] = sc

    def index_loop(jt, carry):
        index_tile(jt, False)
        return carry

    lax.fori_loop(0, i, index_loop, 0)
    index_tile(i, True)

    n_tiles = i + 1

    def sweep(fn, init):
        return lax.fori_loop(0, n_tiles, lambda j, a: fn(pl.multiple_of(j * t, t), t, a), init)

    def count(pred):
        def body(r0, n, cnt):
            x = sc_ref[pl.ds(r0, n), :]
            return cnt + _fold8(jnp.where(pred(x, r0 + key_l), 1.0, 0.0), jnp.add)
        return jnp.sum(sweep(body, jnp.zeros((8, t), F32)), axis=0, keepdims=True)

    def count_below(mid):
        def body(r0, n, cnt):
            d = sc_ref[pl.ds(r0, n), :] - mid
            return cnt + _fold8(lax.shift_right_arithmetic(lax.bitcast_convert_type(d, I32), 31), jnp.add)
        return -jnp.sum(sweep(body, jnp.zeros((8, t), I32)), axis=0, keepdims=True)

    def value_pass(b, v):
        cand = v | jnp.left_shift(jnp.int32(1), 31 - b)
        tot = n_tiles * t - count_below(_pattern_to_float(cand))
        return jnp.where(tot >= DSA_TOPK, cand, v)

    v = lax.fori_loop(0, 32, value_pass, jnp.zeros((1, t), I32))
    floor = _pattern_to_float(v)

    def nearest(pred, fill, op):
        def body(r0, n, acc):
            x = sc_ref[pl.ds(r0, n), :]
            return op(acc, _fold8(jnp.where(pred(x), x, fill), op))
        red = jnp.min if op is jnp.minimum else jnp.max
        return red(sweep(body, jnp.full((8, t), fill, F32)), axis=0, keepdims=True)

    def ranks(th):
        def body(r0, n, cnt):
            x = sc_ref[pl.ds(r0, n), :]
            return (cnt[0] + _fold8(jnp.where(x > th, 1.0, 0.0), jnp.add),
                    cnt[1] + _fold8(jnp.where(x >= th, 1.0, 0.0), jnp.add))
        gt, ge = sweep(body, (jnp.zeros((8, t), F32), jnp.zeros((8, t), F32)))
        return jnp.sum(gt, axis=0, keepdims=True), jnp.sum(ge, axis=0, keepdims=True)

    thr = nearest(lambda x: x >= floor, jnp.inf, jnp.minimum)
    n_gt, n_ge = ranks(thr)

    def off_rank(carry):
        _, gt, ge = carry
        return jnp.max(jnp.where((gt >= DSA_TOPK) | (ge < DSA_TOPK), 1, 0)) > 0

    def step_thr(carry):
        th, gt, ge = carry
        up = nearest(lambda x: x > th, jnp.inf, jnp.minimum)
        down = nearest(lambda x: x < th, -jnp.inf, jnp.maximum)
        th = jnp.where(gt >= DSA_TOPK, up, jnp.where(ge < DSA_TOPK, down, th))
        return (th,) + ranks(th)

    thr, n_gt, n_ge = lax.while_loop(off_rank, step_thr, (thr, n_gt, n_ge))
    need = DSA_TOPK - n_gt
    trim = n_ge > DSA_TOPK

    def index_pass(b, jc):
        cand = jc | jnp.left_shift(jnp.int32(1), 13 - b)
        tot = count(lambda x, key_g: (x == thr) & (key_g < cand))
        return jnp.where(tot < need, cand, jc)

    any_trim = jnp.max(jnp.where(trim, 1, 0)) > 0
    jc = lax.cond(any_trim,
                  lambda: lax.fori_loop(0, 14, index_pass, jnp.zeros((1, t), I32)),
                  lambda: jnp.zeros((1, t), I32))
    jcut = jnp.where(trim, jc, jnp.int32(2 ** 30))

    dv = d + DSA_VPAD
    m_ref[...] = jnp.full(m_ref.shape, NEG_BIG, F32)
    acc_ref[...] = jnp.zeros(acc_ref.shape, F32)

    def attend_tile(jt, masked):
        ks = pl.ds(pl.multiple_of(jt * t, t), t)
        x = sc_ref[ks, :]
        key_g = jt * t + key_l
        sel = (x > thr) | ((x == thr) & (key_g <= jcut))
        if masked:
            sel = sel & (key_g <= qry_g)
        bias = jnp.where(sel, 0.0, NEG_BIG)
        def logits(h):
            return jnp.dot(k_ref[ks, h * d:(h + 1) * d], qt_ref[h * d:(h + 1) * d, :], preferred_element_type=F32) + bias

        lgs = {h: logits(h) for h in range(DSA_QK_AHEAD)}
        for h in range(DSA_HEADS):
            if h + DSA_QK_AHEAD < DSA_HEADS:
                lgs[h + DSA_QK_AHEAD] = logits(h + DSA_QK_AHEAD)
            hv = slice(h * dv, (h + 1) * dv)
            m_old = m_ref[h:h + 1, :]
            m_new = jnp.maximum(m_old, jnp.max(_fold8(lgs[h], jnp.maximum), axis=0, keepdims=True))
            alpha = jnp.exp2(m_old - m_new)
            pexp = jnp.exp2(lgs[h] - m_new).astype(BF16)
            m_ref[h:h + 1, :] = m_new
            acc_ref[hv, :] = alpha * acc_ref[hv, :] + jnp.dot(vt_ref[jt, hv, :], pexp, preferred_element_type=F32)

    def attend_loop(jt, carry):
        attend_tile(jt, False)
        return carry

    lax.fori_loop(0, i, attend_loop, 0)
    attend_tile(i, True)
    outs = [acc_ref[h * dv:h * dv + d, :] / acc_ref[h * dv + d:h * dv + d + 1, :] for h in range(DSA_HEADS)]
    o_ref[...] = jnp.concatenate(outs, axis=0).T.astype(BF16)


def _dsa_main(q_t, qi_t, w_t, k_r, v_t, ki_b):
    b, l, _ = k_r.shape
    t = DSA_TILE

    def rows_t(n):
        return pl.BlockSpec((None, n, t), lambda bi, qi: (bi, 0, qi))

    def whole(shape):
        nd = len(shape)
        return pl.BlockSpec((None,) + shape, lambda bi, qi: (bi,) + (0,) * nd, pipeline_mode=pl.Buffered(1))

    return pl.pallas_call(
        _dsa_body,
        grid=(b, l // t),
        in_specs=[rows_t(BW), rows_t(2 * LANE), rows_t(8),
                  whole((l, BW)), whole((l // t, DSA_VROWS, t)), whole((l, LANE))],
        out_specs=pl.BlockSpec((None, t, BW), lambda bi, qi: (bi, qi, 0)),
        out_shape=jax.ShapeDtypeStruct((b, l, BW), BF16),
        scratch_shapes=[
            pltpu.VMEM((l, t), F32),
            pltpu.VMEM((DSA_HEADS, t), F32),
            pltpu.VMEM((DSA_VROWS, t), F32),
        ],
        compiler_params=_cparams(("parallel", "arbitrary")),
        name="dsa_main",
    )(q_t, qi_t, w_t, k_r, v_t, ki_b)


def _hgrn_body(q_ref, f_ref, i_ref, g_ref, lb_ref, ng_ref, o_ref, st_ref, obuf, *, tt):
    ck = HG_CHUNK

    @pl.when(pl.program_id(1) == 0)
    def _():
        st_ref[...] = jnp.zeros(st_ref.shape, F32)

    lb = lb_ref[...]
    f = f_ref[...]
    qq = _silu(q_ref[...])
    log_f = jnp.log(lb + (1.0 - lb) * jax.nn.sigmoid(f))
    key = (1.0 - lb) * jax.nn.sigmoid(-f)
    v = i_ref[...]

    r = lax.broadcasted_iota(I32, (tt, tt), 0)
    c = lax.broadcasted_iota(I32, (tt, tt), 1)
    blk_causal = ((r // ck) == (c // ck)) & (r >= c)
    cum = jnp.dot(blk_causal.astype(F32), log_f, precision=lax.Precision.HIGHEST, preferred_element_type=F32)
    q_dec = qq * jnp.exp(cum)
    k_dec = key * jnp.exp(-cum)
    q_dec_b = q_dec.astype(BF16)
    k_dec_b = k_dec.astype(BF16)
    v_b = v.astype(BF16)

    for h in range(HG_HEADS):
        hs = slice(h * HG_KDIM, (h + 1) * HG_KDIM)
        att = lax.dot_general(q_dec_b[:, hs], k_dec_b[:, hs], (((1,), (1,)), ((), ())), preferred_element_type=F32)
        att = jnp.where(blk_causal, att, 0.0).astype(BF16)
        obuf[:, hs] = jnp.dot(att, v_b[:, hs], preferred_element_type=F32)

    for cc in range(tt // ck):
        rows = slice(cc * ck, (cc + 1) * ck)
        last = cum[cc * ck + ck - 1:cc * ck + ck, :]
        k_end = (key[rows] * jnp.exp(last - cum[rows])).astype(BF16)
        dec = jnp.exp(last)
        for h in range(HG_HEADS):
            hs = slice(h * HG_KDIM, (h + 1) * HG_KDIM)
            st = st_ref[h]
            o_inter = lax.dot_general(q_dec_b[rows, hs], st.astype(BF16), (((1,), (1,)), ((), ())),
                                      preferred_element_type=F32)
            obuf[rows, hs] = obuf[rows, hs] + o_inter
            upd = lax.dot_general(v_b[rows, hs], k_end[:, hs], (((0,), (0,)), ((), ())),
                                  preferred_element_type=F32)
            st_ref[h] = st * dec[:, hs] + upd

    gate = jax.nn.sigmoid(g_ref[...])
    for h in range(HG_HEADS):
        hs = slice(h * HG_VDIM, (h + 1) * HG_VDIM)
        o = obuf[:, hs]
        on = o * lax.rsqrt(jnp.mean(o * o, axis=-1, keepdims=True) + NORM_EPS) * ng_ref[:, hs]
        o_ref[:, hs] = (on * gate[:, hs]).astype(BF16)


def _hgrn(p3, lb, norm_g, tt=256):
    b, l, _ = p3.shape

    def col(start):
        return pl.BlockSpec((None, tt, BW), lambda bi, ti: (bi, ti, start // BW))

    return pl.pallas_call(
        functools.partial(_hgrn_body, tt=tt),
        grid=(b, l // tt),
        in_specs=[col(C_HGQ), col(C_HGF), col(C_HGI), col(C_HGG), _const_spec((1, BW)), _const_spec((1, BW))],
        out_specs=pl.BlockSpec((None, tt, BW), lambda bi, ti: (bi, ti, 0)),
        out_shape=jax.ShapeDtypeStruct((b, l, BW), BF16),
        scratch_shapes=[pltpu.VMEM((HG_HEADS, HG_VDIM, HG_KDIM), F32), pltpu.VMEM((tt, BW), F32)],
        compiler_params=_cparams(("parallel", "arbitrary")),
        name="hgrn",
    )(p3, p3, p3, p3, lb.reshape(1, BW), norm_g.reshape(1, BW))


def _merge_body(x_ref, h_ref, ya_ref, yb_ref, yc_ref, yd_ref, wg_ref, wb_ref, wo_ref, o_ref):
    h = h_ref[...]
    merged = jnp.zeros(x_ref.shape, F32)
    for g, y_ref in enumerate((ya_ref, yb_ref, yc_ref, yd_ref)):
        gate = jax.nn.sigmoid(jnp.dot(h, wg_ref[:, g * D_MODEL:(g + 1) * D_MODEL], preferred_element_type=F32))
        merged = merged + gate * jnp.dot(y_ref[...], wb_ref[g], preferred_element_type=F32)
    o_ref[...] = x_ref[...] + jnp.dot(merged.astype(BF16), wo_ref[...], preferred_element_type=F32)


def _merge(x, h, ys, w_gate, w_branch, w_out, tm=512):
    t = x.shape[0]
    row = lambda w: pl.BlockSpec((tm, w), lambda i: (i, 0))
    return pl.pallas_call(
        _merge_body,
        grid=(t // tm,),
        in_specs=[row(D_MODEL), row(D_MODEL), row(BW), row(BW), row(BW), row(BW),
                  _const_spec((D_MODEL, N_BRANCHES * D_MODEL)),
                  _const_spec((N_BRANCHES, BW, D_MODEL)),
                  _const_spec((D_MODEL, D_MODEL))],
        out_specs=row(D_MODEL),
        out_shape=jax.ShapeDtypeStruct((t, D_MODEL), F32),
        compiler_params=_cparams(("parallel",)),
        name="merge",
    )(x, h, *ys, w_gate, w_branch, w_out)


def _pad_w_in(w_in):
    o = 0
    seg = {}
    for name, width in (("mz", BW), ("mx", BW), ("mbc", 2 * LANE), ("mdt", M2_HEADS), ("s5", BW),
                        ("dq", BW), ("dk", BW), ("dv", BW), ("dqi", 2 * LANE), ("dki", DSA_IDX_DIM),
                        ("dwi", DSA_IDX_HEADS), ("hq", BW), ("hf", BW), ("hi", BW), ("hg", BW)):
        seg[name] = w_in[:, o:o + width]
        o += width
    z = lambda n: jnp.zeros((D_MODEL, n), w_in.dtype)
    cols = [seg["hq"], seg["hf"], seg["hi"], seg["hg"], seg["dq"], seg["dk"], seg["dv"], seg["mz"], seg["mx"],
            seg["s5"], seg["dqi"], seg["mbc"],
            seg["dki"], seg["dwi"], z(LANE - DSA_IDX_DIM - DSA_IDX_HEADS),
            seg["mdt"], z(LANE - M2_HEADS)]
    return jnp.concatenate(cols, axis=1)


def kernel(x, positions, ffn1_norm, ffn1_w1, ffn1_w3, ffn1_w2, mix_norm, w_in, w_gate, w_branch, w_out, m2_conv_w, m2_conv_b, m2_dt_bias, m2_a_log, m2_d, m2_norm, s5_lam_re, s5_lam_im, s5_b_re, s5_b_im, s5_c_re, s5_c_im, s5_d, s5_log_dt, s5_w_glu, dsa_q_norm, dsa_k_norm, hg_gamma, hg_norm, ffn2_norm, ffn2_w1, ffn2_w3, ffn2_w2):
    b, l, _ = x.shape
    t = b * l
    depth = w_in.shape[0]
    lbs = jnp.cumsum(jax.nn.softmax(hg_gamma.astype(F32), axis=0), axis=0)
    lbs = lbs - lbs[0]
    xf = x.reshape(t, D_MODEL)
    w_pad = jax.vmap(_pad_w_in)(w_in).astype(BF16)
    s5_tabs = jax.vmap(_s5_tables)(s5_lam_re, s5_lam_im, s5_b_re, s5_b_im, s5_c_re, s5_c_im, s5_log_dt)
    for li in range(depth):
        xf = _ffn(xf, ffn1_norm[li], ffn1_w1[li].astype(BF16), ffn1_w3[li].astype(BF16), ffn1_w2[li].astype(BF16))
        p, h = _proj(xf, mix_norm[li], w_pad[li])
        p3 = p.reshape(b, l, NP)
        y_a = _mamba(p3, m2_conv_w[li], m2_conv_b[li], m2_dt_bias[li], m2_a_log[li], m2_d[li], m2_norm[li])
        y_b = _s5(p, b, l, [tb[li] for tb in s5_tabs], s5_d[li], s5_w_glu[li].astype(BF16))
        q_t, k_r, v_t, ki_b, qi_t, w_t = _dsa_prep(p3, positions, dsa_q_norm[li], dsa_k_norm[li])
        y_c = _dsa_main(q_t, qi_t, w_t, k_r, v_t, ki_b)
        y_d = _hgrn(p3, lbs[li], hg_norm[li])
        ys = (y_a.reshape(t, BW), y_b, y_c.reshape(t, BW), y_d.reshape(t, BW))
        xf = _merge(xf, h, ys, w_gate[li].astype(BF16), w_branch[li].astype(BF16), w_out[li].astype(BF16))
        xf = _ffn(xf, ffn2_norm[li], ffn2_w1[li].astype(BF16), ffn2_w3[li].astype(BF16), ffn2_w2[li].astype(BF16))
    return xf.reshape(b, l, D_MODEL)
```

```python
import functools
import math

import jax
import jax.numpy as jnp
from jax import lax
from jax.experimental import pallas as pl
from jax.experimental.pallas import tpu as pltpu

F32 = jnp.float32
BF16 = jnp.bfloat16
I32 = jnp.int32

D_MODEL = 1024
N_BRANCHES = 4
BW = D_MODEL // 2
D_FF = 2816
ROPE_THETA = 10000.0
NORM_EPS = 1e-6

M2_HEAD_DIM = 64
M2_HEADS = BW // M2_HEAD_DIM
M2_GROUPS = 2
M2_STATE = 64
M2_CONV = 4
M2_CHUNK = 128

S5_GROUP = 16
S5_GROUPS = BW // S5_GROUP
S5_STATE = 64
S5_CHUNK = 16
S5_GB = 8

DSA_HEAD_DIM = 64
DSA_HEADS = BW // DSA_HEAD_DIM
DSA_IDX_HEADS = 4
DSA_IDX_DIM = 64
DSA_TOPK = 256

HG_HEADS = 4
HG_KDIM = 128
HG_VDIM = BW // HG_HEADS
HG_CHUNK = 32

C_HGQ, C_HGF, C_HGI, C_HGG = 0, 512, 1024, 1536
C_DQ, C_DK, C_DV = 2048, 2560, 3072
C_MZ, C_MX = 3584, 4096
C_S5 = 4608
C_DQI = 5120
C_MBC = 5376
C_DKW = 5632
C_MDT = 5760
NP = 5888

LANE = 128
INT_MIN = -(2 ** 31)
NEG_BIG = -1e30
MASKED_SCORE = -3e38
DSA_TILE = 256
DSA_QK_AHEAD = 8
DSA_VPAD = 16
DSA_VROWS = DSA_HEADS * (DSA_HEAD_DIM + DSA_VPAD)
VMEM_LIMIT = 56 * 1024 * 1024


def _cparams(sem):
    return pltpu.CompilerParams(dimension_semantics=sem, vmem_limit_bytes=VMEM_LIMIT)


def _const_spec(shape):
    nd = len(shape)
    return pl.BlockSpec(shape, lambda *_: (0,) * nd, pipeline_mode=pl.Buffered(1))


def _layer_spec(shape, li):
    nd = len(shape)
    return pl.BlockSpec((None,) + shape, lambda *_: (li,) + (0,) * nd, pipeline_mode=pl.Buffered(1))


def _silu(x):
    return x * jax.nn.sigmoid(x)


def _ffn_body(x_ref, g_ref, w1_ref, w3_ref, w2_ref, o_ref, *, ff_chunk):
    x = x_ref[...]
    n = x * lax.rsqrt(jnp.mean(x * x, axis=-1, keepdims=True) + NORM_EPS) * g_ref[...]
    n = n.astype(BF16)
    acc = jnp.zeros(x.shape, F32)
    for c in range(D_FF // ff_chunk):
        sl = slice(c * ff_chunk, (c + 1) * ff_chunk)
        h1 = jnp.dot(n, w1_ref[:, sl], preferred_element_type=F32)
        h3 = jnp.dot(n, w3_ref[:, sl], preferred_element_type=F32)
        g = (_silu(h1) * h3).astype(BF16)
        acc = acc + jnp.dot(g, w2_ref[sl, :], preferred_element_type=F32)
    o_ref[...] = x + 0.5 * acc


def _ffn(x, gain, w1, w3, w2, li, tm=512, ff_chunk=256):
    t = x.shape[0]
    return pl.pallas_call(
        functools.partial(_ffn_body, ff_chunk=ff_chunk),
        grid=(t // tm,),
        in_specs=[
            pl.BlockSpec((tm, D_MODEL), lambda i: (i, 0)),
            _const_spec((1, D_MODEL)),
            _layer_spec((D_MODEL, D_FF), li),
            _layer_spec((D_MODEL, D_FF), li),
            _layer_spec((D_FF, D_MODEL), li),
        ],
        out_specs=pl.BlockSpec((tm, D_MODEL), lambda i: (i, 0)),
        out_shape=jax.ShapeDtypeStruct((t, D_MODEL), F32),
        compiler_params=_cparams(("parallel",)),
        name="ffn",
    )(x, gain.reshape(1, D_MODEL), w1, w3, w2)


def _proj_body(x_ref, g_ref, w_ref, p_ref, h_ref):
    x = x_ref[...]
    h = x * lax.rsqrt(jnp.mean(x * x, axis=-1, keepdims=True) + NORM_EPS) * g_ref[...]
    hb = h.astype(BF16)
    h_ref[...] = hb
    p_ref[...] = jnp.dot(hb, w_ref[...], preferred_element_type=F32)


def _proj(x, gain, w_pad, li, tm=256):
    t = x.shape[0]
    return pl.pallas_call(
        _proj_body,
        grid=(t // tm,),
        in_specs=[
            pl.BlockSpec((tm, D_MODEL), lambda i: (i, 0)),
            _const_spec((1, D_MODEL)),
            _layer_spec((D_MODEL, NP), li),
        ],
        out_specs=[
            pl.BlockSpec((tm, NP), lambda i: (i, 0)),
            pl.BlockSpec((tm, D_MODEL), lambda i: (i, 0)),
        ],
        out_shape=[
            jax.ShapeDtypeStruct((t, NP), F32),
            jax.ShapeDtypeStruct((t, D_MODEL), BF16),
        ],
        compiler_params=_cparams(("parallel",)),
        name="proj",
    )(x, gain.reshape(1, D_MODEL), w_pad)


def _mamba_body(z_ref, x_ref, bc_ref, dt_ref, cwx_ref, cbx_ref, cwbc_ref, cbbc_ref, dtb_ref, a_ref,
                dvec_ref, ng_ref, o_ref, xbuf, bcbuf, st_ref, ybuf, *, tt):
    q = M2_CHUNK
    pad = 8

    @pl.when(pl.program_id(1) == 0)
    def _():
        xbuf[0:pad, :] = jnp.zeros((pad, BW), F32)
        bcbuf[0:pad, :] = jnp.zeros((pad, 2 * LANE), F32)
        st_ref[...] = jnp.zeros(st_ref.shape, F32)

    xbuf[pad:pad + tt, :] = x_ref[...]
    bcbuf[pad:pad + tt, :] = bc_ref[...]

    def conv(buf, cw_ref, cb_ref):
        acc = cb_ref[...]
        for w in range(M2_CONV):
            off = pad - (M2_CONV - 1) + w
            acc = acc + cw_ref[w:w + 1, :] * buf[off:off + tt, :]
        return _silu(acc)

    xs = conv(xbuf, cwx_ref, cbx_ref)
    bc = conv(bcbuf, cwbc_ref, cbbc_ref)
    xbuf[0:pad, :] = xbuf[tt:tt + pad, :]
    bcbuf[0:pad, :] = bcbuf[tt:tt + pad, :]

    dtr = dt_ref[...] + dtb_ref[...]
    dt = jnp.maximum(dtr, 0.0) + jnp.log(1.0 + jnp.exp(-jnp.abs(dtr)))
    ad = dt * a_ref[...]

    rq = lax.broadcasted_iota(I32, (q, q), 0)
    cq = lax.broadcasted_iota(I32, (q, q), 1)
    causal = rq >= cq
    tril = causal.astype(F32)

    for c in range(tt // q):
        rows = slice(c * q, (c + 1) * q)
        a_cs = jnp.dot(tril, ad[rows], precision=lax.Precision.HIGHEST, preferred_element_type=F32)
        a_cs_t = a_cs.T
        xs_c = xs[rows]
        dt_c = dt[rows]
        for g in range(M2_GROUPS):
            b_g = bc[rows, g * M2_STATE:(g + 1) * M2_STATE]
            c_g = bc[rows, LANE + g * M2_STATE:LANE + (g + 1) * M2_STATE]
            b_gt = b_g.T
            c_gb = c_g.astype(BF16)
            gmat = jnp.dot(c_gb, b_gt.astype(BF16), preferred_element_type=F32)
            for hh in range(M2_HEADS // M2_GROUPS):
                h = g * (M2_HEADS // M2_GROUPS) + hh
                col = a_cs[:, h:h + 1]
                row = a_cs_t[h:h + 1, :]
                a_last = a_cs[q - 1:q, h:h + 1]
                decay = jnp.exp(jnp.where(causal, col - row, -jnp.inf))
                xdt = (xs_c[:, h * M2_HEAD_DIM:(h + 1) * M2_HEAD_DIM] * dt_c[:, h:h + 1]).astype(BF16)
                y_diag = jnp.dot((gmat * decay).astype(BF16), xdt, preferred_element_type=F32)
                st_in = st_ref[h]
                y_off = jnp.dot(c_gb, st_in.astype(BF16), preferred_element_type=F32) * jnp.exp(col)
                bw_t = (b_gt * jnp.exp(a_last - row)).astype(BF16)
                st_ref[h] = jnp.exp(a_last) * st_in + jnp.dot(bw_t, xdt, preferred_element_type=F32)
                ybuf[rows, h * M2_HEAD_DIM:(h + 1) * M2_HEAD_DIM] = y_diag + y_off

    y = ybuf[...] + dvec_ref[...] * xs
    yz = y * _silu(z_ref[...])
    o_ref[...] = (yz * lax.rsqrt(jnp.mean(yz * yz, axis=-1, keepdims=True) + NORM_EPS) * ng_ref[...]).astype(BF16)


def _mamba(p3, conv_w, conv_b, dt_bias, a_log, d_skip, norm_g, tt=256):
    b, l, _ = p3.shape
    cwx, cwbc = conv_w[:, :BW], conv_w[:, BW:]
    cbx, cbbc = conv_b[:BW].reshape(1, BW), conv_b[BW:].reshape(1, 2 * LANE)
    dtb = jnp.zeros((1, LANE), F32).at[0, :M2_HEADS].set(dt_bias)
    a = jnp.zeros((1, LANE), F32).at[0, :M2_HEADS].set(-jnp.exp(a_log))
    dvec = jnp.repeat(d_skip, M2_HEAD_DIM).reshape(1, BW)

    def col(width, start):
        return pl.BlockSpec((None, tt, width), lambda bi, ti: (bi, ti, start // width))

    return pl.pallas_call(
        functools.partial(_mamba_body, tt=tt),
        grid=(b, l // tt),
        in_specs=[
            col(BW, C_MZ), col(BW, C_MX), col(2 * LANE, C_MBC), col(LANE, C_MDT),
            _const_spec((M2_CONV, BW)), _const_spec((1, BW)),
            _const_spec((M2_CONV, 2 * LANE)), _const_spec((1, 2 * LANE)),
            _const_spec((1, LANE)), _const_spec((1, LANE)),
            _const_spec((1, BW)), _const_spec((1, BW)),
        ],
        out_specs=pl.BlockSpec((None, tt, BW), lambda bi, ti: (bi, ti, 0)),
        out_shape=jax.ShapeDtypeStruct((b, l, BW), BF16),
        scratch_shapes=[
            pltpu.VMEM((tt + 8, BW), F32),
            pltpu.VMEM((tt + 8, 2 * LANE), F32),
            pltpu.VMEM((M2_HEADS, M2_STATE, M2_HEAD_DIM), F32),
            pltpu.VMEM((tt, BW), F32),
        ],
        compiler_params=_cparams(("parallel", "arbitrary")),
        name="mamba",
    )(p3, p3, p3, p3, cwx, cbx, cwbc, cbbc, dtb, a, dvec, norm_g.reshape(1, BW))


def _s5_tables(lam_re, lam_im, b_re, b_im, c_re, c_im, log_dt):
    ck, gs, ps = S5_CHUNK, S5_GROUP, S5_STATE
    step = jnp.exp(log_dt)[:, None]
    ldt_re, ldt_im = lam_re * step, lam_im * step
    mag = jnp.exp(ldt_re)
    ab_re, ab_im = mag * jnp.cos(ldt_im), mag * jnp.sin(ldt_im)
    den = lam_re * lam_re + lam_im * lam_im
    q_re = ((ab_re - 1.0) * lam_re + ab_im * lam_im) / den
    q_im = (ab_im * lam_re - (ab_re - 1.0) * lam_im) / den
    bb_re = q_re[..., None] * b_re - q_im[..., None] * b_im
    bb_im = q_re[..., None] * b_im + q_im[..., None] * b_re
    n = jnp.arange(ck + 1, dtype=F32)[:, None, None]
    pm = jnp.exp(n * ldt_re[None])
    pw_re, pw_im = pm * jnp.cos(n * ldt_im[None]), pm * jnp.sin(n * ldt_im[None])
    ab_r = pw_re[..., None] * bb_re[None] - pw_im[..., None] * bb_im[None]
    ab_i = pw_re[..., None] * bb_im[None] + pw_im[..., None] * bb_re[None]
    kk = jnp.einsum("gip,ngpj->ngij", c_re, ab_r) - jnp.einsum("gip,ngpj->ngij", c_im, ab_i)
    tq = jnp.arange(ck)
    lag = tq[None, :] - tq[:, None]
    kl = jnp.where((lag >= 0)[:, :, None, None, None], kk[jnp.clip(lag, 0, ck)], 0.0)
    m = kl.transpose(2, 0, 4, 1, 3).reshape(S5_GROUPS, ck * gs, ck * gs)
    rev = ck - 1 - tq
    w_re = ab_r[rev].transpose(1, 0, 3, 2).reshape(S5_GROUPS, ck * gs, ps)
    w_im = ab_i[rev].transpose(1, 0, 3, 2).reshape(S5_GROUPS, ck * gs, ps)
    ca_r = c_re[None] * pw_re[1:, :, None, :] - c_im[None] * pw_im[1:, :, None, :]
    ca_i = c_re[None] * pw_im[1:, :, None, :] + c_im[None] * pw_re[1:, :, None, :]
    v_re = ca_r.transpose(1, 3, 0, 2).reshape(S5_GROUPS, ps, ck * gs)
    v_im = (-ca_i).transpose(1, 3, 0, 2).reshape(S5_GROUPS, ps, ck * gs)
    a16_re = pw_re[ck].reshape(S5_GROUPS // S5_GB, 1, S5_GB * ps)
    a16_im = pw_im[ck].reshape(S5_GROUPS // S5_GB, 1, S5_GB * ps)
    return (m.astype(BF16), w_re.astype(BF16), w_im.astype(BF16), v_re.astype(BF16), v_im.astype(BF16),
            a16_re, a16_im)


def _s5_perm():
    ck, gb, gs = S5_CHUNK, S5_GB, S5_GROUP
    r = jnp.arange(ck * gb * gs)
    dest = ((r % (gb * gs)) // gs) * (ck * gs) + (r // (gb * gs)) * gs + r % gs
    return (dest[:, None] == r[None, :]).astype(BF16)


def _s5_body(u_ref, perm_ref, m_ref, wre_ref, wim_ref, vre_ref, vim_ref, are_ref, aim_ref, y_ref,
             xre, xim, sre, sim, ybuf, *, lc):
    p, rw, bw = S5_STATE, S5_CHUNK * S5_GROUP, S5_GB * S5_GROUP
    u = jnp.concatenate([u_ref[pl.ds(jt, lc, stride=S5_CHUNK), :] for jt in range(S5_CHUNK)], axis=1).astype(BF16)
    ug = jnp.dot(u, perm_ref[...], preferred_element_type=F32).astype(BF16)
    for g in range(S5_GB):
        ugg = ug[:, g * rw:(g + 1) * rw]
        xre[:, g * p:(g + 1) * p] = jnp.dot(ugg, wre_ref[g], preferred_element_type=F32)
        xim[:, g * p:(g + 1) * p] = jnp.dot(ugg, wim_ref[g], preferred_element_type=F32)
    a_re = are_ref[...]
    a_im = aim_ref[...]

    def step(c, carry):
        s_re, s_im = carry
        sre[pl.ds(c, 1), :] = s_re
        sim[pl.ds(c, 1), :] = s_im
        n_re = a_re * s_re - a_im * s_im + xre[pl.ds(c, 1), :]
        n_im = a_re * s_im + a_im * s_re + xim[pl.ds(c, 1), :]
        return n_re, n_im

    zero = jnp.zeros((1, S5_GB * p), F32)
    lax.fori_loop(0, lc, step, (zero, zero))
    for g in range(S5_GB):
        y = jnp.dot(ug[:, g * rw:(g + 1) * rw], m_ref[g], preferred_element_type=F32)
        y = y + jnp.dot(sre[:, g * p:(g + 1) * p].astype(BF16), vre_ref[g], preferred_element_type=F32)
        y = y + jnp.dot(sim[:, g * p:(g + 1) * p].astype(BF16), vim_ref[g], preferred_element_type=F32)
        ybuf[:, g * rw:(g + 1) * rw] = y.astype(BF16)
    y_rows = lax.dot_general(ybuf[...], perm_ref[...], (((1,), (1,)), ((), ())), preferred_element_type=F32)
    for jt in range(S5_CHUNK):
        y_ref[pl.ds(jt, lc, stride=S5_CHUNK), :] = y_rows[:, jt * bw:(jt + 1) * bw]


def _s5_core(p3, perm, tables):
    b, l, _ = p3.shape
    m, w_re, w_im, v_re, v_im, a_re, a_im = tables
    gb, sw, gw, bw = S5_GB, S5_GB * S5_STATE, S5_CHUNK * S5_GROUP, S5_GB * S5_GROUP
    nb, lc, rw = S5_GROUPS // S5_GB, l // S5_CHUNK, S5_CHUNK * S5_GB * S5_GROUP

    def grp(shape):
        return pl.BlockSpec((gb,) + shape, lambda bi, gi: (gi, 0, 0))

    def blk(shape):
        return pl.BlockSpec((None,) + shape, lambda bi, gi: (gi, 0, 0))

    return pl.pallas_call(
        functools.partial(_s5_body, lc=lc),
        grid=(b, nb),
        in_specs=[
            pl.BlockSpec((None, l, bw), lambda bi, gi: (bi, 0, C_S5 // bw + gi)),
            _const_spec((rw, rw)),
            grp((gw, gw)), grp((gw, S5_STATE)), grp((gw, S5_STATE)), grp((S5_STATE, gw)), grp((S5_STATE, gw)),
            blk((1, sw)), blk((1, sw)),
        ],
        out_specs=pl.BlockSpec((None, l, bw), lambda bi, gi: (bi, 0, gi)),
        out_shape=jax.ShapeDtypeStruct((b, l, BW), F32),
        scratch_shapes=[pltpu.VMEM((lc, sw), F32) for _ in range(4)] + [pltpu.VMEM((lc, rw), BF16)],
        compiler_params=_cparams(("parallel", "parallel")),
        name="s5_core",
    )(p3, perm, m, w_re, w_im, v_re, v_im, a_re, a_im)


def _s5_post_body(y_ref, u_ref, d_ref, w_ref, o_ref):
    y = y_ref[...] + d_ref[...] * u_ref[...]
    c = math.sqrt(2.0 / math.pi)
    ge = 0.5 * y * (1.0 + jnp.tanh(c * (y + 0.044715 * (y * y * y))))
    g2 = jnp.dot(ge.astype(BF16), w_ref[...], preferred_element_type=F32)
    o_ref[...] = (g2[:, :BW] * jax.nn.sigmoid(g2[:, BW:])).astype(BF16)


def _s5_post(y_ssm, p, d_skip, w_glu, tm=512):
    t = y_ssm.shape[0]
    return pl.pallas_call(
        _s5_post_body,
        grid=(t // tm,),
        in_specs=[
            pl.BlockSpec((tm, BW), lambda i: (i, 0)),
            pl.BlockSpec((tm, BW), lambda i: (i, C_S5 // BW)),
            _const_spec((1, BW)),
            _const_spec((BW, 2 * BW)),
        ],
        out_specs=pl.BlockSpec((tm, BW), lambda i: (i, 0)),
        out_shape=jax.ShapeDtypeStruct((t, BW), BF16),
        compiler_params=_cparams(("parallel",)),
        name="s5_post",
    )(y_ssm, p, d_skip.reshape(1, BW), w_glu)


def _s5(p, b, l, tables, d_skip, w_glu):
    y_ssm = _s5_core(p.reshape(b, l, NP), _s5_perm(), tables).reshape(b * l, BW)
    return _s5_post(y_ssm, p, d_skip, w_glu)


def _dsa_prep_body(q_ref, k_ref, v_ref, qi_ref, kw_ref, pos_ref, qn_ref, kn_ref, freq_ref, sgn_ref, ones_ref,
                   qt_ref, ko_ref, vt_ref, kio_ref, qit_ref, wt_ref):
    ang = pos_ref[...].astype(F32) * freq_ref[...]
    reps = BW // LANE
    cos = jnp.concatenate([jnp.cos(ang)] * reps, axis=1)
    sin = jnp.concatenate([jnp.sin(ang) * sgn_ref[...]] * reps, axis=1)
    lane = lax.broadcasted_iota(I32, cos.shape, 1)
    first_half = (lane % DSA_HEAD_DIM) < (DSA_HEAD_DIM // 2)

    def norm_rope(x, gain, scale):
        x2 = x * x
        hi = x2.astype(BF16)
        lo = (x2 - hi.astype(F32)).astype(BF16)
        ms = (jnp.dot(hi, ones_ref[...], preferred_element_type=F32)
              + jnp.dot(lo, ones_ref[...], preferred_element_type=F32))
        xn = x * lax.rsqrt(ms * (1.0 / DSA_HEAD_DIM) + NORM_EPS) * gain
        half = DSA_HEAD_DIM // 2
        partner = jnp.where(first_half, pltpu.roll(xn, BW - half, 1), pltpu.roll(xn, half, 1))
        return (xn * cos + partner * sin) * scale

    qt_ref[...] = norm_rope(q_ref[...], qn_ref[...], DSA_HEAD_DIM ** -0.5 * math.log2(math.e)).T.astype(BF16)
    ko_ref[...] = norm_rope(k_ref[...], kn_ref[...], 1.0).astype(BF16)
    vt = v_ref[...].T
    ones = jnp.ones((DSA_VPAD, vt.shape[1]), F32)
    pieces = []
    for h in range(DSA_HEADS):
        pieces += [vt[h * DSA_HEAD_DIM:(h + 1) * DSA_HEAD_DIM, :], ones]
    vt_ref[...] = jnp.concatenate(pieces, axis=0).astype(BF16)
    kw = kw_ref[...]
    kio_ref[...] = kw.astype(BF16)
    qit_ref[...] = qi_ref[...].T.astype(BF16)
    wt = kw.T[DSA_IDX_DIM:DSA_IDX_DIM + 8, :]
    wt_ref[...] = wt * (DSA_IDX_HEADS ** -0.5 * DSA_IDX_DIM ** -0.5)


def _dsa_prep(p3, pos, q_norm, k_norm):
    b, l, _ = p3.shape
    tt = DSA_TILE
    half = DSA_HEAD_DIM // 2
    inv_freq = ROPE_THETA ** (-jnp.arange(half, dtype=F32) / half)
    freq = jnp.tile(jnp.concatenate([inv_freq, inv_freq]), LANE // DSA_HEAD_DIM).reshape(1, LANE)
    sgn = jnp.tile(jnp.concatenate([-jnp.ones(half, F32), jnp.ones(half, F32)]), LANE // DSA_HEAD_DIM).reshape(1, LANE)
    hid = jnp.arange(BW) // DSA_HEAD_DIM
    ones_bd = (hid[:, None] == hid[None, :]).astype(BF16)
    qn = jnp.tile(q_norm, DSA_HEADS).reshape(1, BW)
    kn = jnp.tile(k_norm, DSA_HEADS).reshape(1, BW)

    def col(width, start):
        return pl.BlockSpec((None, tt, width), lambda bi, ti: (bi, ti, start // width))

    def rows_t(n):
        return pl.BlockSpec((None, n, tt), lambda bi, ti: (bi, 0, ti))

    return pl.pallas_call(
        _dsa_prep_body,
        grid=(b, l // tt),
        in_specs=[
            col(BW, C_DQ), col(BW, C_DK), col(BW, C_DV), col(2 * LANE, C_DQI), col(LANE, C_DKW),
            pl.BlockSpec((None, tt, 1), lambda bi, ti: (bi, ti, 0)),
            _const_spec((1, BW)), _const_spec((1, BW)), _const_spec((1, LANE)), _const_spec((1, LANE)),
            _const_spec((BW, BW)),
        ],
        out_specs=[
            rows_t(BW),
            pl.BlockSpec((None, tt, BW), lambda bi, ti: (bi, ti, 0)),
            pl.BlockSpec((None, None, DSA_VROWS, tt), lambda bi, ti: (bi, ti, 0, 0)),
            pl.BlockSpec((None, tt, LANE), lambda bi, ti: (bi, ti, 0)),
            rows_t(2 * LANE),
            rows_t(8),
        ],
        out_shape=[
            jax.ShapeDtypeStruct((b, BW, l), BF16),
            jax.ShapeDtypeStruct((b, l, BW), BF16),
            jax.ShapeDtypeStruct((b, l // tt, DSA_VROWS, tt), BF16),
            jax.ShapeDtypeStruct((b, l, LANE), BF16),
            jax.ShapeDtypeStruct((b, 2 * LANE, l), BF16),
            jax.ShapeDtypeStruct((b, 8, l), F32),
        ],
        compiler_params=_cparams(("parallel", "parallel")),
        name="dsa_prep",
    )(p3, p3, p3, p3, p3, pos.reshape(b, l, 1), qn, kn, freq, sgn, ones_bd)


def _fold8(w, op):
    parts = [w[8 * k:8 * (k + 1), :] for k in range(w.shape[0] // 8)]
    while len(parts) > 1:
        parts = [op(parts[2 * k], parts[2 * k + 1]) for k in range(len(parts) // 2)]
    return parts[0]


def _pattern_to_float(u):
    bits = jnp.where(u < 0, u ^ INT_MIN, ~u)
    return lax.bitcast_convert_type(bits, F32)


def _dsa_body(qt_ref, qit_ref, wt_ref, k_ref, vt_ref, ki_ref, o_ref, sc_ref, m_ref, acc_ref):
    t = DSA_TILE
    d = DSA_HEAD_DIM
    i = pl.program_id(1)
    key_l = lax.broadcasted_iota(I32, (t, t), 0)
    qry_g = i * t + lax.broadcasted_iota(I32, (t, t), 1)

    def index_tile(jt, masked):
        ks = pl.ds(pl.multiple_of(jt * t, t), t)
        kib = ki_ref[ks, 0:DSA_IDX_DIM]
        sc = jnp.zeros((t, t), F32)
        for h in range(DSA_IDX_HEADS):
            dd = jnp.dot(kib, qit_ref[h * DSA_IDX_DIM:(h + 1) * DSA_IDX_DIM, :], preferred_element_type=F32)
            sc = sc + jnp.maximum(dd, 0.0) * wt_ref[h:h + 1, :]
        if masked:
            sc = jnp.where(jt * t + key_l <= qry_g, sc, MASKED_SCORE)
        sc_ref[ks, :] = sc

    def index_loop(jt, carry):
        index_tile(jt, False)
        return carry

    lax.fori_loop(0, i, index_loop, 0)
    index_tile(i, True)

    n_tiles = i + 1

    def sweep(fn, init):
        return lax.fori_loop(0, n_tiles, lambda j, a: fn(pl.multiple_of(j * t, t), t, a), init)

    def count(pred):
        def body(r0, n, cnt):
            x = sc_ref[pl.ds(r0, n), :]
            return cnt + _fold8(jnp.where(pred(x, r0 + key_l), 1.0, 0.0), jnp.add)
        return jnp.sum(sweep(body, jnp.zeros((8, t), F32)), axis=0, keepdims=True)

    def value_pass(b, v):
        cand = v | jnp.left_shift(jnp.int32(1), 31 - b)
        mid = _pattern_to_float(cand)
        tot = count(lambda x, key_g: x >= mid)
        return jnp.where(tot >= DSA_TOPK, cand, v)

    v = lax.fori_loop(0, 32, value_pass, jnp.zeros((1, t), I32))
    floor = _pattern_to_float(v)

    def nearest(pred, fill, op):
        def body(r0, n, acc):
            x = sc_ref[pl.ds(r0, n), :]
            return op(acc, _fold8(jnp.where(pred(x), x, fill), op))
        red = jnp.min if op is jnp.minimum else jnp.max
        return red(sweep(body, jnp.full((8, t), fill, F32)), axis=0, keepdims=True)

    def ranks(th):
        def body(r0, n, cnt):
            x = sc_ref[pl.ds(r0, n), :]
            return (cnt[0] + _fold8(jnp.where(x > th, 1.0, 0.0), jnp.add),
                    cnt[1] + _fold8(jnp.where(x >= th, 1.0, 0.0), jnp.add))
        gt, ge = sweep(body, (jnp.zeros((8, t), F32), jnp.zeros((8, t), F32)))
        return jnp.sum(gt, axis=0, keepdims=True), jnp.sum(ge, axis=0, keepdims=True)

    thr = nearest(lambda x: x >= floor, jnp.inf, jnp.minimum)
    n_gt, n_ge = ranks(thr)

    def off_rank(carry):
        _, gt, ge = carry
        return jnp.max(jnp.where((gt >= DSA_TOPK) | (ge < DSA_TOPK), 1, 0)) > 0

    def step_thr(carry):
        th, gt, ge = carry
        up = nearest(lambda x: x > th, jnp.inf, jnp.minimum)
        down = nearest(lambda x: x < th, -jnp.inf, jnp.maximum)
        th = jnp.where(gt >= DSA_TOPK, up, jnp.where(ge < DSA_TOPK, down, th))
        return (th,) + ranks(th)

    thr, n_gt, n_ge = lax.while_loop(off_rank, step_thr, (thr, n_gt, n_ge))
    need = DSA_TOPK - n_gt

    dv = d + DSA_VPAD
    m_ref[...] = jnp.full(m_ref.shape, NEG_BIG, F32)
    acc_ref[...] = jnp.zeros(acc_ref.shape, F32)
    incl = (lax.broadcasted_iota(I32, (t, t), 1) <= key_l).astype(BF16)

    def attend_tile(jt, masked, ties_before):
        ks = pl.ds(pl.multiple_of(jt * t, t), t)
        x = sc_ref[ks, :]
        tie = x == thr
        rank = ties_before + jnp.dot(incl, jnp.where(tie, 1.0, 0.0).astype(BF16), preferred_element_type=F32)
        sel = (x > thr) | (tie & (rank <= need))
        if masked:
            sel = sel & (jt * t + key_l <= qry_g)
        bias = jnp.where(sel, 0.0, NEG_BIG)
        def logits(h):
            return jnp.dot(k_ref[ks, h * d:(h + 1) * d], qt_ref[h * d:(h + 1) * d, :], preferred_element_type=F32) + bias

        lgs = {h: logits(h) for h in range(DSA_QK_AHEAD)}
        for h in range(DSA_HEADS):
            if h + DSA_QK_AHEAD < DSA_HEADS:
                lgs[h + DSA_QK_AHEAD] = logits(h + DSA_QK_AHEAD)
            hv = slice(h * dv, (h + 1) * dv)
            m_old = m_ref[h:h + 1, :]
            m_new = jnp.maximum(m_old, jnp.max(_fold8(lgs[h], jnp.maximum), axis=0, keepdims=True))
            alpha = jnp.exp2(m_old - m_new)
            pexp = jnp.exp2(lgs[h] - m_new).astype(BF16)
            m_ref[h:h + 1, :] = m_new
            acc_ref[hv, :] = alpha * acc_ref[hv, :] + jnp.dot(vt_ref[jt, hv, :], pexp, preferred_element_type=F32)
        return rank[t - 1:t, :]

    ties = lax.fori_loop(0, i, lambda jt, tb: attend_tile(jt, False, tb), jnp.zeros((1, t), F32))
    attend_tile(i, True, ties)
    outs = [acc_ref[h * dv:h * dv + d, :] / acc_ref[h * dv + d:h * dv + d + 1, :] for h in range(DSA_HEADS)]
    o_ref[...] = jnp.concatenate(outs, axis=0).T.astype(BF16)


def _dsa_main(q_t, qi_t, w_t, k_r, v_t, ki_b):
    b, l, _ = k_r.shape
    t = DSA_TILE

    def rows_t(n):
        return pl.BlockSpec((None, n, t), lambda bi, qi: (bi, 0, qi))

    def whole(shape):
        nd = len(shape)
        return pl.BlockSpec((None,) + shape, lambda bi, qi: (bi,) + (0,) * nd, pipeline_mode=pl.Buffered(1))

    return pl.pallas_call(
        _dsa_body,
        grid=(b, l // t),
        in_specs=[rows_t(BW), rows_t(2 * LANE), rows_t(8),
                  whole((l, BW)), whole((l // t, DSA_VROWS, t)), whole((l, LANE))],
        out_specs=pl.BlockSpec((None, t, BW), lambda bi, qi: (bi, qi, 0)),
        out_shape=jax.ShapeDtypeStruct((b, l, BW), BF16),
        scratch_shapes=[
            pltpu.VMEM((l, t), F32),
            pltpu.VMEM((DSA_HEADS, t), F32),
            pltpu.VMEM((DSA_VROWS, t), F32),
        ],
        compiler_params=_cparams(("parallel", "arbitrary")),
        name="dsa_main",
    )(q_t, qi_t, w_t, k_r, v_t, ki_b)


def _hgrn_body(q_ref, f_ref, i_ref, g_ref, lb_ref, ng_ref, o_ref, st_ref, obuf, *, tt):
    ck = HG_CHUNK

    @pl.when(pl.program_id(1) == 0)
    def _():
        st_ref[...] = jnp.zeros(st_ref.shape, F32)

    lb = lb_ref[...]
    f = f_ref[...]
    qq = _silu(q_ref[...])
    log_f = jnp.log(lb + (1.0 - lb) * jax.nn.sigmoid(f))
    key = (1.0 - lb) * jax.nn.sigmoid(-f)
    v = i_ref[...]

    r = lax.broadcasted_iota(I32, (tt, tt), 0)
    c = lax.broadcasted_iota(I32, (tt, tt), 1)
    blk_causal = ((r // ck) == (c // ck)) & (r >= c)
    cum = jnp.dot(blk_causal.astype(F32), log_f, precision=lax.Precision.HIGHEST, preferred_element_type=F32)
    q_dec = qq * jnp.exp(cum)
    k_dec = key * jnp.exp(-cum)
    q_dec_b = q_dec.astype(BF16)
    k_dec_b = k_dec.astype(BF16)
    v_b = v.astype(BF16)

    for h in range(HG_HEADS):
        hs = slice(h * HG_KDIM, (h + 1) * HG_KDIM)
        att = lax.dot_general(q_dec_b[:, hs], k_dec_b[:, hs], (((1,), (1,)), ((), ())), preferred_element_type=F32)
        att = jnp.where(blk_causal, att, 0.0).astype(BF16)
        obuf[:, hs] = jnp.dot(att, v_b[:, hs], preferred_element_type=F32)

    for cc in range(tt // ck):
        rows = slice(cc * ck, (cc + 1) * ck)
        last = cum[cc * ck + ck - 1:cc * ck + ck, :]
        k_end = (key[rows] * jnp.exp(last - cum[rows])).astype(BF16)
        dec = jnp.exp(last)
        for h in range(HG_HEADS):
            hs = slice(h * HG_KDIM, (h + 1) * HG_KDIM)
            st = st_ref[h]
            o_inter = lax.dot_general(q_dec_b[rows, hs], st.astype(BF16), (((1,), (1,)), ((), ())),
                                      preferred_element_type=F32)
            obuf[rows, hs] = obuf[rows, hs] + o_inter
            upd = lax.dot_general(v_b[rows, hs], k_end[:, hs], (((0,), (0,)), ((), ())),
                                  preferred_element_type=F32)
            st_ref[h] = st * dec[:, hs] + upd

    gate = jax.nn.sigmoid(g_ref[...])
    for h in range(HG_HEADS):
        hs = slice(h * HG_VDIM, (h + 1) * HG_VDIM)
        o = obuf[:, hs]
        on = o * lax.rsqrt(jnp.mean(o * o, axis=-1, keepdims=True) + NORM_EPS) * ng_ref[:, hs]
        o_ref[:, hs] = (on * gate[:, hs]).astype(BF16)


def _hgrn(p3, lb, norm_g, tt=256):
    b, l, _ = p3.shape

    def col(start):
        return pl.BlockSpec((None, tt, BW), lambda bi, ti: (bi, ti, start // BW))

    return pl.pallas_call(
        functools.partial(_hgrn_body, tt=tt),
        grid=(b, l // tt),
        in_specs=[col(C_HGQ), col(C_HGF), col(C_HGI), col(C_HGG), _const_spec((1, BW)), _const_spec((1, BW))],
        out_specs=pl.BlockSpec((None, tt, BW), lambda bi, ti: (bi, ti, 0)),
        out_shape=jax.ShapeDtypeStruct((b, l, BW), BF16),
        scratch_shapes=[pltpu.VMEM((HG_HEADS, HG_VDIM, HG_KDIM), F32), pltpu.VMEM((tt, BW), F32)],
        compiler_params=_cparams(("parallel", "arbitrary")),
        name="hgrn",
    )(p3, p3, p3, p3, lb.reshape(1, BW), norm_g.reshape(1, BW))


def _merge_body(x_ref, h_ref, ya_ref, yb_ref, yc_ref, yd_ref, wg_ref, wb_ref, wo_ref, o_ref):
    h = h_ref[...]
    merged = jnp.zeros(x_ref.shape, F32)
    for g, y_ref in enumerate((ya_ref, yb_ref, yc_ref, yd_ref)):
        gate = jax.nn.sigmoid(jnp.dot(h, wg_ref[:, g * D_MODEL:(g + 1) * D_MODEL], preferred_element_type=F32))
        merged = merged + gate * jnp.dot(y_ref[...], wb_ref[g], preferred_element_type=F32)
    o_ref[...] = x_ref[...] + jnp.dot(merged.astype(BF16), wo_ref[...], preferred_element_type=F32)


def _merge(x, h, ys, w_gate, w_branch, w_out, li, tm=512):
    t = x.shape[0]
    row = lambda w: pl.BlockSpec((tm, w), lambda i: (i, 0))
    return pl.pallas_call(
        _merge_body,
        grid=(t // tm,),
        in_specs=[row(D_MODEL), row(D_MODEL), row(BW), row(BW), row(BW), row(BW),
                  _layer_spec((D_MODEL, N_BRANCHES * D_MODEL), li),
                  _layer_spec((N_BRANCHES, BW, D_MODEL), li),
                  _layer_spec((D_MODEL, D_MODEL), li)],
        out_specs=row(D_MODEL),
        out_shape=jax.ShapeDtypeStruct((t, D_MODEL), F32),
        compiler_params=_cparams(("parallel",)),
        name="merge",
    )(x, h, *ys, w_gate, w_branch, w_out)


def _pad_w_in(w_in):
    o = 0
    seg = {}
    for name, width in (("mz", BW), ("mx", BW), ("mbc", 2 * LANE), ("mdt", M2_HEADS), ("s5", BW),
                        ("dq", BW), ("dk", BW), ("dv", BW), ("dqi", 2 * LANE), ("dki", DSA_IDX_DIM),
                        ("dwi", DSA_IDX_HEADS), ("hq", BW), ("hf", BW), ("hi", BW), ("hg", BW)):
        seg[name] = w_in[:, o:o + width]
        o += width
    z = lambda n: jnp.zeros((D_MODEL, n), w_in.dtype)
    cols = [seg["hq"], seg["hf"], seg["hi"], seg["hg"], seg["dq"], seg["dk"], seg["dv"], seg["mz"], seg["mx"],
            seg["s5"], seg["dqi"], seg["mbc"],
            seg["dki"], seg["dwi"], z(LANE - DSA_IDX_DIM - DSA_IDX_HEADS),
            seg["mdt"], z(LANE - M2_HEADS)]
    return jnp.concatenate(cols, axis=1)


def kernel(x, positions, ffn1_norm, ffn1_w1, ffn1_w3, ffn1_w2, mix_norm, w_in, w_gate, w_branch, w_out, m2_conv_w, m2_conv_b, m2_dt_bias, m2_a_log, m2_d, m2_norm, s5_lam_re, s5_lam_im, s5_b_re, s5_b_im, s5_c_re, s5_c_im, s5_d, s5_log_dt, s5_w_glu, dsa_q_norm, dsa_k_norm, hg_gamma, hg_norm, ffn2_norm, ffn2_w1, ffn2_w3, ffn2_w2):
    b, l, _ = x.shape
    t = b * l
    depth = w_in.shape[0]
    lbs = jnp.cumsum(jax.nn.softmax(hg_gamma.astype(F32), axis=0), axis=0)
    lbs = lbs - lbs[0]
    xf = x.reshape(t, D_MODEL)
    w_pad = jax.vmap(_pad_w_in)(w_in).astype(BF16)
    s5_tabs = jax.vmap(_s5_tables)(s5_lam_re, s5_lam_im, s5_b_re, s5_b_im, s5_c_re, s5_c_im, s5_log_dt)
    f1 = [w.astype(BF16) for w in (ffn1_w1, ffn1_w3, ffn1_w2)]
    f2 = [w.astype(BF16) for w in (ffn2_w1, ffn2_w3, ffn2_w2)]
    w_gate_b, w_branch_b, w_out_b = w_gate.astype(BF16), w_branch.astype(BF16), w_out.astype(BF16)
    for li in range(depth):
        xf = _ffn(xf, ffn1_norm[li], *f1, li)
        p, h = _proj(xf, mix_norm[li], w_pad, li)
        p3 = p.reshape(b, l, NP)
        y_a = _mamba(p3, m2_conv_w[li], m2_conv_b[li], m2_dt_bias[li], m2_a_log[li], m2_d[li], m2_norm[li])
        y_b = _s5(p, b, l, [tb[li] for tb in s5_tabs], s5_d[li], s5_w_glu[li].astype(BF16))
        q_t, k_r, v_t, ki_b, qi_t, w_t = _dsa_prep(p3, positions, dsa_q_norm[li], dsa_k_norm[li])
        y_c = _dsa_main(q_t, qi_t, w_t, k_r, v_t, ki_b)
        y_d = _hgrn(p3, lbs[li], hg_norm[li])
        ys = (y_a.reshape(t, BW), y_b, y_c.reshape(t, BW), y_d.reshape(t, BW))
        xf = _merge(xf, h, ys, w_gate_b, w_branch_b, w_out_b, li)
        xf = _ffn(xf, ffn2_norm[li], *f2, li)
    return xf.reshape(b, l, D_MODEL)
```

```python
import functools
import math

import jax
import jax.numpy as jnp
from jax import lax
from jax.experimental import pallas as pl
from jax.experimental.pallas import tpu as pltpu

F32 = jnp.float32
BF16 = jnp.bfloat16
I32 = jnp.int32

D_MODEL = 1024
N_BRANCHES = 4
BW = D_MODEL // 2
D_FF = 2816
ROPE_THETA = 10000.0
NORM_EPS = 1e-6

M2_HEAD_DIM = 64
M2_HEADS = BW // M2_HEAD_DIM
M2_GROUPS = 2
M2_STATE = 64
M2_CONV = 4
M2_CHUNK = 128

S5_GROUP = 16
S5_GROUPS = BW // S5_GROUP
S5_STATE = 64
S5_CHUNK = 16
S5_GB = 8

DSA_HEAD_DIM = 64
DSA_HEADS = BW // DSA_HEAD_DIM
DSA_IDX_HEADS = 4
DSA_IDX_DIM = 64
DSA_TOPK = 256

HG_HEADS = 4
HG_KDIM = 128
HG_VDIM = BW // HG_HEADS
HG_CHUNK = 32

C_HGQ, C_HGF, C_HGI, C_HGG = 0, 512, 1024, 1536
C_DQ, C_DK, C_DV = 2048, 2560, 3072
C_MZ, C_MX = 3584, 4096
C_S5 = 4608
C_DQI = 5120
C_MBC = 5376
C_DKW = 5632
C_MDT = 5760
NP = 5888

LANE = 128
INT_MIN = -(2 ** 31)
NEG_BIG = -1e30
MASKED_SCORE = -3e38
DSA_TILE = 256
DSA_QK_AHEAD = 8
DSA_VPAD = 16
DSA_VROWS = DSA_HEADS * (DSA_HEAD_DIM + DSA_VPAD)
VMEM_LIMIT = 56 * 1024 * 1024


def _cparams(sem):
    return pltpu.CompilerParams(dimension_semantics=sem, vmem_limit_bytes=VMEM_LIMIT)


def _const_spec(shape):
    nd = len(shape)
    return pl.BlockSpec(shape, lambda *_: (0,) * nd, pipeline_mode=pl.Buffered(1))


def _layer_spec(shape, li):
    nd = len(shape)
    return pl.BlockSpec((None,) + shape, lambda *_: (li,) + (0,) * nd, pipeline_mode=pl.Buffered(1))


def _silu(x):
    return x * jax.nn.sigmoid(x)


def _ffn_body(x_ref, g_ref, w1_ref, w3_ref, w2_ref, o_ref, *, ff_chunk):
    x = x_ref[...]
    n = x * lax.rsqrt(jnp.mean(x * x, axis=-1, keepdims=True) + NORM_EPS) * g_ref[...]
    n = n.astype(BF16)
    acc = jnp.zeros(x.shape, F32)
    for c in range(D_FF // ff_chunk):
        sl = slice(c * ff_chunk, (c + 1) * ff_chunk)
        h1 = jnp.dot(n, w1_ref[:, sl], preferred_element_type=F32)
        h3 = jnp.dot(n, w3_ref[:, sl], preferred_element_type=F32)
        g = (_silu(h1) * h3).astype(BF16)
        acc = acc + jnp.dot(g, w2_ref[sl, :], preferred_element_type=F32)
    o_ref[...] = x + 0.5 * acc


def _ffn(x, gain, w1, w3, w2, li, tm=512, ff_chunk=256):
    t = x.shape[0]
    return pl.pallas_call(
        functools.partial(_ffn_body, ff_chunk=ff_chunk),
        grid=(t // tm,),
        in_specs=[
            pl.BlockSpec((tm, D_MODEL), lambda i: (i, 0)),
            _const_spec((1, D_MODEL)),
            _layer_spec((D_MODEL, D_FF), li),
            _layer_spec((D_MODEL, D_FF), li),
            _layer_spec((D_FF, D_MODEL), li),
        ],
        out_specs=pl.BlockSpec((tm, D_MODEL), lambda i: (i, 0)),
        out_shape=jax.ShapeDtypeStruct((t, D_MODEL), F32),
        compiler_params=_cparams(("parallel",)),
        name="ffn",
    )(x, gain.reshape(1, D_MODEL), w1, w3, w2)


def _proj_body(x_ref, g_ref, w_ref, p_ref, h_ref):
    x = x_ref[...]
    h = x * lax.rsqrt(jnp.mean(x * x, axis=-1, keepdims=True) + NORM_EPS) * g_ref[...]
    hb = h.astype(BF16)
    h_ref[...] = hb
    p_ref[...] = jnp.dot(hb, w_ref[...], preferred_element_type=F32)


def _proj(x, gain, w_pad, li, tm=256):
    t = x.shape[0]
    return pl.pallas_call(
        _proj_body,
        grid=(t // tm,),
        in_specs=[
            pl.BlockSpec((tm, D_MODEL), lambda i: (i, 0)),
            _const_spec((1, D_MODEL)),
            _layer_spec((D_MODEL, NP), li),
        ],
        out_specs=[
            pl.BlockSpec((tm, NP), lambda i: (i, 0)),
            pl.BlockSpec((tm, D_MODEL), lambda i: (i, 0)),
        ],
        out_shape=[
            jax.ShapeDtypeStruct((t, NP), F32),
            jax.ShapeDtypeStruct((t, D_MODEL), BF16),
        ],
        compiler_params=_cparams(("parallel",)),
        name="proj",
    )(x, gain.reshape(1, D_MODEL), w_pad)


def _mamba_body(z_ref, x_ref, bc_ref, dt_ref, cwx_ref, cbx_ref, cwbc_ref, cbbc_ref, dtb_ref, a_ref,
                dvec_ref, ng_ref, o_ref, xbuf, bcbuf, st_ref, ybuf, *, tt):
    q = M2_CHUNK
    pad = 8

    @pl.when(pl.program_id(1) == 0)
    def _():
        xbuf[0:pad, :] = jnp.zeros((pad, BW), F32)
        bcbuf[0:pad, :] = jnp.zeros((pad, 2 * LANE), F32)
        st_ref[...] = jnp.zeros(st_ref.shape, F32)

    xbuf[pad:pad + tt, :] = x_ref[...]
    bcbuf[pad:pad + tt, :] = bc_ref[...]

    def conv(buf, cw_ref, cb_ref):
        acc = cb_ref[...]
        for w in range(M2_CONV):
            off = pad - (M2_CONV - 1) + w
            acc = acc + cw_ref[w:w + 1, :] * buf[off:off + tt, :]
        return _silu(acc)

    xs = conv(xbuf, cwx_ref, cbx_ref)
    bc = conv(bcbuf, cwbc_ref, cbbc_ref)
    xbuf[0:pad, :] = xbuf[tt:tt + pad, :]
    bcbuf[0:pad, :] = bcbuf[tt:tt + pad, :]

    dtr = dt_ref[...] + dtb_ref[...]
    dt = jnp.maximum(dtr, 0.0) + jnp.log(1.0 + jnp.exp(-jnp.abs(dtr)))
    ad = dt * a_ref[...]

    rq = lax.broadcasted_iota(I32, (q, q), 0)
    cq = lax.broadcasted_iota(I32, (q, q), 1)
    causal = rq >= cq
    tril = causal.astype(F32)

    for c in range(tt // q):
        rows = slice(c * q, (c + 1) * q)
        a_cs = jnp.dot(tril, ad[rows], precision=lax.Precision.HIGHEST, preferred_element_type=F32)
        a_cs_t = a_cs.T
        xs_c = xs[rows]
        dt_c = dt[rows]
        for g in range(M2_GROUPS):
            b_g = bc[rows, g * M2_STATE:(g + 1) * M2_STATE]
            c_g = bc[rows, LANE + g * M2_STATE:LANE + (g + 1) * M2_STATE]
            b_gt = b_g.T
            c_gb = c_g.astype(BF16)
            gmat = jnp.dot(c_gb, b_gt.astype(BF16), preferred_element_type=F32)
            for hh in range(M2_HEADS // M2_GROUPS):
                h = g * (M2_HEADS // M2_GROUPS) + hh
                col = a_cs[:, h:h + 1]
                row = a_cs_t[h:h + 1, :]
                a_last = a_cs[q - 1:q, h:h + 1]
                decay = jnp.exp(jnp.where(causal, col - row, -jnp.inf))
                xdt = (xs_c[:, h * M2_HEAD_DIM:(h + 1) * M2_HEAD_DIM] * dt_c[:, h:h + 1]).astype(BF16)
                y_diag = jnp.dot((gmat * decay).astype(BF16), xdt, preferred_element_type=F32)
                st_in = st_ref[h]
                y_off = jnp.dot(c_gb, st_in.astype(BF16), preferred_element_type=F32) * jnp.exp(col)
                bw_t = (b_gt * jnp.exp(a_last - row)).astype(BF16)
                st_ref[h] = jnp.exp(a_last) * st_in + jnp.dot(bw_t, xdt, preferred_element_type=F32)
                ybuf[rows, h * M2_HEAD_DIM:(h + 1) * M2_HEAD_DIM] = y_diag + y_off

    y = ybuf[...] + dvec_ref[...] * xs
    yz = y * _silu(z_ref[...])
    o_ref[...] = (yz * lax.rsqrt(jnp.mean(yz * yz, axis=-1, keepdims=True) + NORM_EPS) * ng_ref[...]).astype(BF16)


def _mamba(p3, conv_w, conv_b, dt_bias, a_log, d_skip, norm_g, tt=256):
    b, l, _ = p3.shape
    cwx, cwbc = conv_w[:, :BW], conv_w[:, BW:]
    cbx, cbbc = conv_b[:BW].reshape(1, BW), conv_b[BW:].reshape(1, 2 * LANE)
    dtb = jnp.zeros((1, LANE), F32).at[0, :M2_HEADS].set(dt_bias)
    a = jnp.zeros((1, LANE), F32).at[0, :M2_HEADS].set(-jnp.exp(a_log))
    dvec = jnp.repeat(d_skip, M2_HEAD_DIM).reshape(1, BW)

    def col(width, start):
        return pl.BlockSpec((None, tt, width), lambda bi, ti: (bi, ti, start // width))

    return pl.pallas_call(
        functools.partial(_mamba_body, tt=tt),
        grid=(b, l // tt),
        in_specs=[
            col(BW, C_MZ), col(BW, C_MX), col(2 * LANE, C_MBC), col(LANE, C_MDT),
            _const_spec((M2_CONV, BW)), _const_spec((1, BW)),
            _const_spec((M2_CONV, 2 * LANE)), _const_spec((1, 2 * LANE)),
            _const_spec((1, LANE)), _const_spec((1, LANE)),
            _const_spec((1, BW)), _const_spec((1, BW)),
        ],
        out_specs=pl.BlockSpec((None, tt, BW), lambda bi, ti: (bi, ti, 0)),
        out_shape=jax.ShapeDtypeStruct((b, l, BW), BF16),
        scratch_shapes=[
            pltpu.VMEM((tt + 8, BW), F32),
            pltpu.VMEM((tt + 8, 2 * LANE), F32),
            pltpu.VMEM((M2_HEADS, M2_STATE, M2_HEAD_DIM), F32),
            pltpu.VMEM((tt, BW), F32),
        ],
        compiler_params=_cparams(("parallel", "arbitrary")),
        name="mamba",
    )(p3, p3, p3, p3, cwx, cbx, cwbc, cbbc, dtb, a, dvec, norm_g.reshape(1, BW))


def _s5_tables(lam_re, lam_im, b_re, b_im, c_re, c_im, log_dt):
    ck, gs, ps = S5_CHUNK, S5_GROUP, S5_STATE
    step = jnp.exp(log_dt)[:, None]
    ldt_re, ldt_im = lam_re * step, lam_im * step
    mag = jnp.exp(ldt_re)
    ab_re, ab_im = mag * jnp.cos(ldt_im), mag * jnp.sin(ldt_im)
    den = lam_re * lam_re + lam_im * lam_im
    q_re = ((ab_re - 1.0) * lam_re + ab_im * lam_im) / den
    q_im = (ab_im * lam_re - (ab_re - 1.0) * lam_im) / den
    bb_re = q_re[..., None] * b_re - q_im[..., None] * b_im
    bb_im = q_re[..., None] * b_im + q_im[..., None] * b_re
    n = jnp.arange(ck + 1, dtype=F32)[:, None, None]
    pm = jnp.exp(n * ldt_re[None])
    pw_re, pw_im = pm * jnp.cos(n * ldt_im[None]), pm * jnp.sin(n * ldt_im[None])
    ab_r = pw_re[..., None] * bb_re[None] - pw_im[..., None] * bb_im[None]
    ab_i = pw_re[..., None] * bb_im[None] + pw_im[..., None] * bb_re[None]
    kk = jnp.einsum("gip,ngpj->ngij", c_re, ab_r) - jnp.einsum("gip,ngpj->ngij", c_im, ab_i)
    tq = jnp.arange(ck)
    lag = tq[None, :] - tq[:, None]
    kl = jnp.where((lag >= 0)[:, :, None, None, None], kk[jnp.clip(lag, 0, ck)], 0.0)
    m = kl.transpose(2, 0, 4, 1, 3).reshape(S5_GROUPS, ck * gs, ck * gs)
    rev = ck - 1 - tq
    w_re = ab_r[rev].transpose(1, 0, 3, 2).reshape(S5_GROUPS, ck * gs, ps)
    w_im = ab_i[rev].transpose(1, 0, 3, 2).reshape(S5_GROUPS, ck * gs, ps)
    ca_r = c_re[None] * pw_re[1:, :, None, :] - c_im[None] * pw_im[1:, :, None, :]
    ca_i = c_re[None] * pw_im[1:, :, None, :] + c_im[None] * pw_re[1:, :, None, :]
    v_re = ca_r.transpose(1, 3, 0, 2).reshape(S5_GROUPS, ps, ck * gs)
    v_im = (-ca_i).transpose(1, 3, 0, 2).reshape(S5_GROUPS, ps, ck * gs)
    a16_re = pw_re[ck].reshape(S5_GROUPS // S5_GB, 1, S5_GB * ps)
    a16_im = pw_im[ck].reshape(S5_GROUPS // S5_GB, 1, S5_GB * ps)
    return (m.astype(BF16), w_re.astype(BF16), w_im.astype(BF16), v_re.astype(BF16), v_im.astype(BF16),
            a16_re, a16_im)


def _s5_perm():
    ck, gb, gs = S5_CHUNK, S5_GB, S5_GROUP
    r = jnp.arange(ck * gb * gs)
    dest = ((r % (gb * gs)) // gs) * (ck * gs) + (r // (gb * gs)) * gs + r % gs
    return (dest[:, None] == r[None, :]).astype(BF16)


def _s5_body(u_ref, perm_ref, m_ref, wre_ref, wim_ref, vre_ref, vim_ref, are_ref, aim_ref, y_ref,
             xre, xim, sre, sim, ybuf, *, lc):
    p, rw, bw = S5_STATE, S5_CHUNK * S5_GROUP, S5_GB * S5_GROUP
    u = jnp.concatenate([u_ref[pl.ds(jt, lc, stride=S5_CHUNK), :] for jt in range(S5_CHUNK)], axis=1).astype(BF16)
    ug = jnp.dot(u, perm_ref[...], preferred_element_type=F32).astype(BF16)
    for g in range(S5_GB):
        ugg = ug[:, g * rw:(g + 1) * rw]
        xre[:, g * p:(g + 1) * p] = jnp.dot(ugg, wre_ref[g], preferred_element_type=F32)
        xim[:, g * p:(g + 1) * p] = jnp.dot(ugg, wim_ref[g], preferred_element_type=F32)
    a_re = are_ref[...]
    a_im = aim_ref[...]

    def step(c, carry):
        s_re, s_im = carry
        sre[pl.ds(c, 1), :] = s_re
        sim[pl.ds(c, 1), :] = s_im
        n_re = a_re * s_re - a_im * s_im + xre[pl.ds(c, 1), :]
        n_im = a_re * s_im + a_im * s_re + xim[pl.ds(c, 1), :]
        return n_re, n_im

    zero = jnp.zeros((1, S5_GB * p), F32)
    lax.fori_loop(0, lc, step, (zero, zero))
    for g in range(S5_GB):
        y = jnp.dot(ug[:, g * rw:(g + 1) * rw], m_ref[g], preferred_element_type=F32)
        y = y + jnp.dot(sre[:, g * p:(g + 1) * p].astype(BF16), vre_ref[g], preferred_element_type=F32)
        y = y + jnp.dot(sim[:, g * p:(g + 1) * p].astype(BF16), vim_ref[g], preferred_element_type=F32)
        ybuf[:, g * rw:(g + 1) * rw] = y.astype(BF16)
    y_rows = lax.dot_general(ybuf[...], perm_ref[...], (((1,), (1,)), ((), ())), preferred_element_type=F32)
    for jt in range(S5_CHUNK):
        y_ref[pl.ds(jt, lc, stride=S5_CHUNK), :] = y_rows[:, jt * bw:(jt + 1) * bw]


def _s5_core(p3, perm, tables):
    b, l, _ = p3.shape
    m, w_re, w_im, v_re, v_im, a_re, a_im = tables
    gb, sw, gw, bw = S5_GB, S5_GB * S5_STATE, S5_CHUNK * S5_GROUP, S5_GB * S5_GROUP
    nb, lc, rw = S5_GROUPS // S5_GB, l // S5_CHUNK, S5_CHUNK * S5_GB * S5_GROUP

    def grp(shape):
        return pl.BlockSpec((gb,) + shape, lambda bi, gi: (gi, 0, 0))

    def blk(shape):
        return pl.BlockSpec((None,) + shape, lambda bi, gi: (gi, 0, 0))

    return pl.pallas_call(
        functools.partial(_s5_body, lc=lc),
        grid=(b, nb),
        in_specs=[
            pl.BlockSpec((None, l, bw), lambda bi, gi: (bi, 0, C_S5 // bw + gi)),
            _const_spec((rw, rw)),
            grp((gw, gw)), grp((gw, S5_STATE)), grp((gw, S5_STATE)), grp((S5_STATE, gw)), grp((S5_STATE, gw)),
            blk((1, sw)), blk((1, sw)),
        ],
        out_specs=pl.BlockSpec((None, l, bw), lambda bi, gi: (bi, 0, gi)),
        out_shape=jax.ShapeDtypeStruct((b, l, BW), F32),
        scratch_shapes=[pltpu.VMEM((lc, sw), F32) for _ in range(4)] + [pltpu.VMEM((lc, rw), BF16)],
        compiler_params=_cparams(("parallel", "parallel")),
        name="s5_core",
    )(p3, perm, m, w_re, w_im, v_re, v_im, a_re, a_im)


def _s5_post_body(y_ref, u_ref, d_ref, w_ref, o_ref):
    y = y_ref[...] + d_ref[...] * u_ref[...]
    c = math.sqrt(2.0 / math.pi)
    ge = 0.5 * y * (1.0 + jnp.tanh(c * (y + 0.044715 * (y * y * y))))
    g2 = jnp.dot(ge.astype(BF16), w_ref[...], preferred_element_type=F32)
    o_ref[...] = (g2[:, :BW] * jax.nn.sigmoid(g2[:, BW:])).astype(BF16)


def _s5_post(y_ssm, p, d_skip, w_glu, tm=512):
    t = y_ssm.shape[0]
    return pl.pallas_call(
        _s5_post_body,
        grid=(t // tm,),
        in_specs=[
            pl.BlockSpec((tm, BW), lambda i: (i, 0)),
            pl.BlockSpec((tm, BW), lambda i: (i, C_S5 // BW)),
            _const_spec((1, BW)),
            _const_spec((BW, 2 * BW)),
        ],
        out_specs=pl.BlockSpec((tm, BW), lambda i: (i, 0)),
        out_shape=jax.ShapeDtypeStruct((t, BW), BF16),
        compiler_params=_cparams(("parallel",)),
        name="s5_post",
    )(y_ssm, p, d_skip.reshape(1, BW), w_glu)


def _s5(p, b, l, tables, d_skip, w_glu):
    y_ssm = _s5_core(p.reshape(b, l, NP), _s5_perm(), tables).reshape(b * l, BW)
    return _s5_post(y_ssm, p, d_skip, w_glu)


def _dsa_prep_body(q_ref, k_ref, v_ref, qi_ref, kw_ref, pos_ref, qn_ref, kn_ref, freq_ref, sgn_ref, ones_ref,
                   qt_ref, ko_ref, vt_ref, kio_ref, qit_ref, wt_ref):
    ang = pos_ref[...].astype(F32) * freq_ref[...]
    reps = BW // LANE
    cos = jnp.concatenate([jnp.cos(ang)] * reps, axis=1)
    sin = jnp.concatenate([jnp.sin(ang) * sgn_ref[...]] * reps, axis=1)
    lane = lax.broadcasted_iota(I32, cos.shape, 1)
    first_half = (lane % DSA_HEAD_DIM) < (DSA_HEAD_DIM // 2)

    def norm_rope(x, gain, scale):
        x2 = x * x
        hi = x2.astype(BF16)
        lo = (x2 - hi.astype(F32)).astype(BF16)
        ms = (jnp.dot(hi, ones_ref[...], preferred_element_type=F32)
              + jnp.dot(lo, ones_ref[...], preferred_element_type=F32))
        xn = x * lax.rsqrt(ms * (1.0 / DSA_HEAD_DIM) + NORM_EPS) * gain
        half = DSA_HEAD_DIM // 2
        partner = jnp.where(first_half, pltpu.roll(xn, BW - half, 1), pltpu.roll(xn, half, 1))
        return (xn * cos + partner * sin) * scale

    qt_ref[...] = norm_rope(q_ref[...], qn_ref[...], DSA_HEAD_DIM ** -0.5 * math.log2(math.e)).T.astype(BF16)
    ko_ref[...] = norm_rope(k_ref[...], kn_ref[...], 1.0).astype(BF16)
    vt = v_ref[...].T
    ones = jnp.ones((DSA_VPAD, vt.shape[1]), F32)
    pieces = []
    for h in range(DSA_HEADS):
        pieces += [vt[h * DSA_HEAD_DIM:(h + 1) * DSA_HEAD_DIM, :], ones]
    vt_ref[...] = jnp.concatenate(pieces, axis=0).astype(BF16)
    kw = kw_ref[...]
    kio_ref[...] = kw.astype(BF16)
    qit_ref[...] = qi_ref[...].T.astype(BF16)
    wt = kw.T[DSA_IDX_DIM:DSA_IDX_DIM + 8, :]
    wt_ref[...] = wt * (DSA_IDX_HEADS ** -0.5 * DSA_IDX_DIM ** -0.5)


def _dsa_prep(p3, pos, q_norm, k_norm):
    b, l, _ = p3.shape
    tt = DSA_TILE
    half = DSA_HEAD_DIM // 2
    inv_freq = ROPE_THETA ** (-jnp.arange(half, dtype=F32) / half)
    freq = jnp.tile(jnp.concatenate([inv_freq, inv_freq]), LANE // DSA_HEAD_DIM).reshape(1, LANE)
    sgn = jnp.tile(jnp.concatenate([-jnp.ones(half, F32), jnp.ones(half, F32)]), LANE // DSA_HEAD_DIM).reshape(1, LANE)
    hid = jnp.arange(BW) // DSA_HEAD_DIM
    ones_bd = (hid[:, None] == hid[None, :]).astype(BF16)
    qn = jnp.tile(q_norm, DSA_HEADS).reshape(1, BW)
    kn = jnp.tile(k_norm, DSA_HEADS).reshape(1, BW)

    def col(width, start):
        return pl.BlockSpec((None, tt, width), lambda bi, ti: (bi, ti, start // width))

    def rows_t(n):
        return pl.BlockSpec((None, n, tt), lambda bi, ti: (bi, 0, ti))

    return pl.pallas_call(
        _dsa_prep_body,
        grid=(b, l // tt),
        in_specs=[
            col(BW, C_DQ), col(BW, C_DK), col(BW, C_DV), col(2 * LANE, C_DQI), col(LANE, C_DKW),
            pl.BlockSpec((None, tt, 1), lambda bi, ti: (bi, ti, 0)),
            _const_spec((1, BW)), _const_spec((1, BW)), _const_spec((1, LANE)), _const_spec((1, LANE)),
            _const_spec((BW, BW)),
        ],
        out_specs=[
            rows_t(BW),
            pl.BlockSpec((None, tt, BW), lambda bi, ti: (bi, ti, 0)),
            pl.BlockSpec((None, None, DSA_VROWS, tt), lambda bi, ti: (bi, ti, 0, 0)),
            pl.BlockSpec((None, tt, LANE), lambda bi, ti: (bi, ti, 0)),
            rows_t(2 * LANE),
            rows_t(8),
        ],
        out_shape=[
            jax.ShapeDtypeStruct((b, BW, l), BF16),
            jax.ShapeDtypeStruct((b, l, BW), BF16),
            jax.ShapeDtypeStruct((b, l // tt, DSA_VROWS, tt), BF16),
            jax.ShapeDtypeStruct((b, l, LANE), BF16),
            jax.ShapeDtypeStruct((b, 2 * LANE, l), BF16),
            jax.ShapeDtypeStruct((b, 8, l), F32),
        ],
        compiler_params=_cparams(("parallel", "parallel")),
        name="dsa_prep",
    )(p3, p3, p3, p3, p3, pos.reshape(b, l, 1), qn, kn, freq, sgn, ones_bd)


def _fold8(w, op):
    parts = [w[8 * k:8 * (k + 1), :] for k in range(w.shape[0] // 8)]
    while len(parts) > 1:
        parts = [op(parts[2 * k], parts[2 * k + 1]) for k in range(len(parts) // 2)]
    return parts[0]


def _pattern_to_float(u):
    bits = jnp.where(u < 0, u ^ INT_MIN, ~u)
    return lax.bitcast_convert_type(bits, F32)


def _round_to_f32_grid(x):
    ef = lax.bitcast_convert_type(x, I32) & 0x7F800000
    up = lax.bitcast_convert_type((150 << 23) - (ef - (127 << 23)), F32)
    down = lax.bitcast_convert_type(ef - (23 << 23), F32)
    y = x * up
    f = jnp.floor(y)
    r = f + jnp.where(y - f >= 0.5, 1.0, 0.0)
    scalable = (ef >= (24 << 23)) & (ef <= (253 << 23))
    return jnp.where(scalable, r * down, x)


def _dsa_body(qt_ref, qit_ref, wt_ref, k_ref, vt_ref, ki_ref, o_ref, sc_ref, scb_ref, m_ref, acc_ref):
    t = DSA_TILE
    d = DSA_HEAD_DIM
    i = pl.program_id(1)
    key_l = lax.broadcasted_iota(I32, (t, t), 0)
    qry_g = i * t + lax.broadcasted_iota(I32, (t, t), 1)

    def index_tile(jt, masked):
        ks = pl.ds(pl.multiple_of(jt * t, t), t)
        kib = ki_ref[ks, 0:DSA_IDX_DIM]
        sc = jnp.zeros((t, t), F32)
        for h in range(DSA_IDX_HEADS):
            dd = jnp.dot(kib, qit_ref[h * DSA_IDX_DIM:(h + 1) * DSA_IDX_DIM, :], preferred_element_type=F32)
            sc = sc + jnp.maximum(dd, 0.0) * wt_ref[h:h + 1, :]
        sc = _round_to_f32_grid(sc)
        if masked:
            sc = jnp.where(jt * t + key_l <= qry_g, sc, MASKED_SCORE)
        sc_ref[ks, :] = sc
        scb_ref[ks, :] = sc.astype(BF16)

    def index_loop(jt, carry):
        index_tile(jt, False)
        return carry

    lax.fori_loop(0, i, index_loop, 0)
    index_tile(i, True)

    n_tiles = i + 1

    def sweep(fn, init):
        return lax.fori_loop(0, n_tiles, lambda j, a: fn(pl.multiple_of(j * t, t), t, a), init)

    def count(pred):
        def body(r0, n, cnt):
            x = sc_ref[pl.ds(r0, n), :]
            return cnt + _fold8(jnp.where(pred(x, r0 + key_l), 1.0, 0.0), jnp.add)
        return jnp.sum(sweep(body, jnp.zeros((8, t), F32)), axis=0, keepdims=True)

    def count_coarse(mid):
        def body(r0, n, cnt):
            w = jnp.where(scb_ref[pl.ds(r0, n), :] >= mid, jnp.ones((), BF16), jnp.zeros((), BF16))
            parts = [w[16 * k:16 * (k + 1), :] for k in range(n // 16)]
            while len(parts) > 1:
                parts = [parts[2 * k] + parts[2 * k + 1] for k in range(len(parts) // 2)]
            return cnt + parts[0].astype(F32)
        return jnp.sum(sweep(body, jnp.zeros((16, t), F32)), axis=0, keepdims=True)

    def coarse_pass(b, v16):
        cand = v16 | jnp.left_shift(jnp.int32(1), 15 - b)
        mid = _pattern_to_float(jnp.left_shift(cand, 16)).astype(BF16)
        return jnp.where(count_coarse(mid) >= DSA_TOPK, cand, v16)

    v16 = lax.fori_loop(0, 16, coarse_pass, jnp.zeros((1, t), I32))

    base = jnp.left_shift(v16 - 1, 16)

    def fine_pass(b, off):
        cand = off | jnp.left_shift(jnp.int32(1), 17 - b)
        mid = _pattern_to_float(base + cand)
        tot = count(lambda x, key_g: x >= mid)
        return jnp.where(tot >= DSA_TOPK, cand, off)

    off = lax.fori_loop(0, 18, fine_pass, jnp.zeros((1, t), I32))
    floor = _pattern_to_float(base + off)

    def nearest(pred, fill, op):
        def body(r0, n, acc):
            x = sc_ref[pl.ds(r0, n), :]
            return op(acc, _fold8(jnp.where(pred(x), x, fill), op))
        red = jnp.min if op is jnp.minimum else jnp.max
        return red(sweep(body, jnp.full((8, t), fill, F32)), axis=0, keepdims=True)

    def ranks(th):
        def body(r0, n, cnt):
            x = sc_ref[pl.ds(r0, n), :]
            return (cnt[0] + _fold8(jnp.where(x > th, 1.0, 0.0), jnp.add),
                    cnt[1] + _fold8(jnp.where(x >= th, 1.0, 0.0), jnp.add))
        gt, ge = sweep(body, (jnp.zeros((8, t), F32), jnp.zeros((8, t), F32)))
        return jnp.sum(gt, axis=0, keepdims=True), jnp.sum(ge, axis=0, keepdims=True)

    thr = nearest(lambda x: x >= floor, jnp.inf, jnp.minimum)
    n_gt, n_ge = ranks(thr)

    def off_rank(carry):
        _, gt, ge = carry
        return jnp.max(jnp.where((gt >= DSA_TOPK) | (ge < DSA_TOPK), 1, 0)) > 0

    def step_thr(carry):
        th, gt, ge = carry
        up = nearest(lambda x: x > th, jnp.inf, jnp.minimum)
        down = nearest(lambda x: x < th, -jnp.inf, jnp.maximum)
        th = jnp.where(gt >= DSA_TOPK, up, jnp.where(ge < DSA_TOPK, down, th))
        return (th,) + ranks(th)

    thr, n_gt, n_ge = lax.while_loop(off_rank, step_thr, (thr, n_gt, n_ge))
    need = DSA_TOPK - n_gt

    dv = d + DSA_VPAD
    m_ref[...] = jnp.full(m_ref.shape, NEG_BIG, F32)
    acc_ref[...] = jnp.zeros(acc_ref.shape, F32)
    incl = (lax.broadcasted_iota(I32, (t, t), 1) <= key_l).astype(BF16)

    def attend_tile(jt, masked, ties_before):
        ks = pl.ds(pl.multiple_of(jt * t, t), t)
        x = sc_ref[ks, :]
        tie = x == thr
        rank = ties_before + jnp.dot(incl, jnp.where(tie, 1.0, 0.0).astype(BF16), preferred_element_type=F32)
        sel = (x > thr) | (tie & (rank <= need))
        if masked:
            sel = sel & (jt * t + key_l <= qry_g)
        bias = jnp.where(sel, 0.0, NEG_BIG)
        def logits(h):
            return jnp.dot(k_ref[ks, h * d:(h + 1) * d], qt_ref[h * d:(h + 1) * d, :], preferred_element_type=F32) + bias

        lgs = {h: logits(h) for h in range(DSA_QK_AHEAD)}
        for h in range(DSA_HEADS):
            if h + DSA_QK_AHEAD < DSA_HEADS:
                lgs[h + DSA_QK_AHEAD] = logits(h + DSA_QK_AHEAD)
            hv = slice(h * dv, (h + 1) * dv)
            m_old = m_ref[h:h + 1, :]
            m_new = jnp.maximum(m_old, jnp.max(_fold8(lgs[h], jnp.maximum), axis=0, keepdims=True))
            alpha = jnp.exp2(m_old - m_new)
            pexp = jnp.exp2(lgs[h] - m_new).astype(BF16)
            m_ref[h:h + 1, :] = m_new
            acc_ref[hv, :] = alpha * acc_ref[hv, :] + jnp.dot(vt_ref[jt, hv, :], pexp, preferred_element_type=F32)
        return rank[t - 1:t, :]

    ties = lax.fori_loop(0, i, lambda jt, tb: attend_tile(jt, False, tb), jnp.zeros((1, t), F32))
    attend_tile(i, True, ties)
    outs = [acc_ref[h * dv:h * dv + d, :] / acc_ref[h * dv + d:h * dv + d + 1, :] for h in range(DSA_HEADS)]
    o_ref[...] = jnp.concatenate(outs, axis=0).T.astype(BF16)


def _dsa_main(q_t, qi_t, w_t, k_r, v_t, ki_b):
    b, l, _ = k_r.shape
    t = DSA_TILE

    def rows_t(n):
        return pl.BlockSpec((None, n, t), lambda bi, qi: (bi, 0, qi))

    def whole(shape):
        nd = len(shape)
        return pl.BlockSpec((None,) + shape, lambda bi, qi: (bi,) + (0,) * nd, pipeline_mode=pl.Buffered(1))

    return pl.pallas_call(
        _dsa_body,
        grid=(b, l // t),
        in_specs=[rows_t(BW), rows_t(2 * LANE), rows_t(8),
                  whole((l, BW)), whole((l // t, DSA_VROWS, t)), whole((l, LANE))],
        out_specs=pl.BlockSpec((None, t, BW), lambda bi, qi: (bi, qi, 0)),
        out_shape=jax.ShapeDtypeStruct((b, l, BW), BF16),
        scratch_shapes=[
            pltpu.VMEM((l, t), F32),
            pltpu.VMEM((l, t), BF16),
            pltpu.VMEM((DSA_HEADS, t), F32),
            pltpu.VMEM((DSA_VROWS, t), F32),
        ],
        compiler_params=_cparams(("parallel", "arbitrary")),
        name="dsa_main",
    )(q_t, qi_t, w_t, k_r, v_t, ki_b)


def _hgrn_body(q_ref, f_ref, i_ref, g_ref, lb_ref, ng_ref, o_ref, st_ref, obuf, *, tt):
    ck = HG_CHUNK

    @pl.when(pl.program_id(1) == 0)
    def _():
        st_ref[...] = jnp.zeros(st_ref.shape, F32)

    lb = lb_ref[...]
    f = f_ref[...]
    qq = _silu(q_ref[...])
    log_f = jnp.log(lb + (1.0 - lb) * jax.nn.sigmoid(f))
    key = (1.0 - lb) * jax.nn.sigmoid(-f)
    v = i_ref[...]

    r = lax.broadcasted_iota(I32, (tt, tt), 0)
    c = lax.broadcasted_iota(I32, (tt, tt), 1)
    blk_causal = ((r // ck) == (c // ck)) & (r >= c)
    cum = jnp.dot(blk_causal.astype(F32), log_f, precision=lax.Precision.HIGHEST, preferred_element_type=F32)
    q_dec = qq * jnp.exp(cum)
    k_dec = key * jnp.exp(-cum)
    q_dec_b = q_dec.astype(BF16)
    k_dec_b = k_dec.astype(BF16)
    v_b = v.astype(BF16)

    for h in range(HG_HEADS):
        hs = slice(h * HG_KDIM, (h + 1) * HG_KDIM)
        att = lax.dot_general(q_dec_b[:, hs], k_dec_b[:, hs], (((1,), (1,)), ((), ())), preferred_element_type=F32)
        att = jnp.where(blk_causal, att, 0.0).astype(BF16)
        obuf[:, hs] = jnp.dot(att, v_b[:, hs], preferred_element_type=F32)

    for cc in range(tt // ck):
        rows = slice(cc * ck, (cc + 1) * ck)
        last = cum[cc * ck + ck - 1:cc * ck + ck, :]
        k_end = (key[rows] * jnp.exp(last - cum[rows])).astype(BF16)
        dec = jnp.exp(last)
        for h in range(HG_HEADS):
            hs = slice(h * HG_KDIM, (h + 1) * HG_KDIM)
            st = st_ref[h]
            o_inter = lax.dot_general(q_dec_b[rows, hs], st.astype(BF16), (((1,), (1,)), ((), ())),
                                      preferred_element_type=F32)
            obuf[rows, hs] = obuf[rows, hs] + o_inter
            upd = lax.dot_general(v_b[rows, hs], k_end[:, hs], (((0,), (0,)), ((), ())),
                                  preferred_element_type=F32)
            st_ref[h] = st * dec[:, hs] + upd

    gate = jax.nn.sigmoid(g_ref[...])
    for h in range(HG_HEADS):
        hs = slice(h * HG_VDIM, (h + 1) * HG_VDIM)
        o = obuf[:, hs]
        on = o * lax.rsqrt(jnp.mean(o * o, axis=-1, keepdims=True) + NORM_EPS) * ng_ref[:, hs]
        o_ref[:, hs] = (on * gate[:, hs]).astype(BF16)


def _hgrn(p3, lb, norm_g, tt=256):
    b, l, _ = p3.shape

    def col(start):
        return pl.BlockSpec((None, tt, BW), lambda bi, ti: (bi, ti, start // BW))

    return pl.pallas_call(
        functools.partial(_hgrn_body, tt=tt),
        grid=(b, l // tt),
        in_specs=[col(C_HGQ), col(C_HGF), col(C_HGI), col(C_HGG), _const_spec((1, BW)), _const_spec((1, BW))],
        out_specs=pl.BlockSpec((None, tt, BW), lambda bi, ti: (bi, ti, 0)),
        out_shape=jax.ShapeDtypeStruct((b, l, BW), BF16),
        scratch_shapes=[pltpu.VMEM((HG_HEADS, HG_VDIM, HG_KDIM), F32), pltpu.VMEM((tt, BW), F32)],
        compiler_params=_cparams(("parallel", "arbitrary")),
        name="hgrn",
    )(p3, p3, p3, p3, lb.reshape(1, BW), norm_g.reshape(1, BW))


def _merge_body(x_ref, h_ref, ya_ref, yb_ref, yc_ref, yd_ref, wg_ref, wb_ref, wo_ref, o_ref):
    h = h_ref[...]
    merged = jnp.zeros(x_ref.shape, F32)
    for g, y_ref in enumerate((ya_ref, yb_ref, yc_ref, yd_ref)):
        gate = jax.nn.sigmoid(jnp.dot(h, wg_ref[:, g * D_MODEL:(g + 1) * D_MODEL], preferred_element_type=F32))
        merged = merged + gate * jnp.dot(y_ref[...], wb_ref[g], preferred_element_type=F32)
    o_ref[...] = x_ref[...] + jnp.dot(merged.astype(BF16), wo_ref[...], preferred_element_type=F32)


def _merge(x, h, ys, w_gate, w_branch, w_out, li, tm=512):
    t = x.shape[0]
    row = lambda w: pl.BlockSpec((tm, w), lambda i: (i, 0))
    return pl.pallas_call(
        _merge_body,
        grid=(t // tm,),
        in_specs=[row(D_MODEL), row(D_MODEL), row(BW), row(BW), row(BW), row(BW),
                  _layer_spec((D_MODEL, N_BRANCHES * D_MODEL), li),
                  _layer_spec((N_BRANCHES, BW, D_MODEL), li),
                  _layer_spec((D_MODEL, D_MODEL), li)],
        out_specs=row(D_MODEL),
        out_shape=jax.ShapeDtypeStruct((t, D_MODEL), F32),
        compiler_params=_cparams(("parallel",)),
        name="merge",
    )(x, h, *ys, w_gate, w_branch, w_out)


def _pad_w_in(w_in):
    o = 0
    seg = {}
    for name, width in (("mz", BW), ("mx", BW), ("mbc", 2 * LANE), ("mdt", M2_HEADS), ("s5", BW),
                        ("dq", BW), ("dk", BW), ("dv", BW), ("dqi", 2 * LANE), ("dki", DSA_IDX_DIM),
                        ("dwi", DSA_IDX_HEADS), ("hq", BW), ("hf", BW), ("hi", BW), ("hg", BW)):
        seg[name] = w_in[:, o:o + width]
        o += width
    z = lambda n: jnp.zeros((D_MODEL, n), w_in.dtype)
    cols = [seg["hq"], seg["hf"], seg["hi"], seg["hg"], seg["dq"], seg["dk"], seg["dv"], seg["mz"], seg["mx"],
            seg["s5"], seg["dqi"], seg["mbc"],
            seg["dki"], seg["dwi"], z(LANE - DSA_IDX_DIM - DSA_IDX_HEADS),
            seg["mdt"], z(LANE - M2_HEADS)]
    return jnp.concatenate(cols, axis=1)


def kernel(x, positions, ffn1_norm, ffn1_w1, ffn1_w3, ffn1_w2, mix_norm, w_in, w_gate, w_branch, w_out, m2_conv_w, m2_conv_b, m2_dt_bias, m2_a_log, m2_d, m2_norm, s5_lam_re, s5_lam_im, s5_b_re, s5_b_im, s5_c_re, s5_c_im, s5_d, s5_log_dt, s5_w_glu, dsa_q_norm, dsa_k_norm, hg_gamma, hg_norm, ffn2_norm, ffn2_w1, ffn2_w3, ffn2_w2):
    b, l, _ = x.shape
    t = b * l
    depth = w_in.shape[0]
    lbs = jnp.cumsum(jax.nn.softmax(hg_gamma.astype(F32), axis=0), axis=0)
    lbs = lbs - lbs[0]
    xf = x.reshape(t, D_MODEL)
    w_pad = jax.vmap(_pad_w_in)(w_in).astype(BF16)
    s5_tabs = jax.vmap(_s5_tables)(s5_lam_re, s5_lam_im, s5_b_re, s5_b_im, s5_c_re, s5_c_im, s5_log_dt)
    f1 = [w.astype(BF16) for w in (ffn1_w1, ffn1_w3, ffn1_w2)]
    f2 = [w.astype(BF16) for w in (ffn2_w1, ffn2_w3, ffn2_w2)]
    w_gate_b, w_branch_b, w_out_b = w_gate.astype(BF16), w_branch.astype(BF16), w_out.astype(BF16)
    for li in range(depth):
        xf = _ffn(xf, ffn1_norm[li], *f1, li)
        p, h = _proj(xf, mix_norm[li], w_pad, li)
        p3 = p.reshape(b, l, NP)
        y_a = _mamba(p3, m2_conv_w[li], m2_conv_b[li], m2_dt_bias[li], m2_a_log[li], m2_d[li], m2_norm[li])
        y_b = _s5(p, b, l, [tb[li] for tb in s5_tabs], s5_d[li], s5_w_glu[li].astype(BF16))
        q_t, k_r, v_t, ki_b, qi_t, w_t = _dsa_prep(p3, positions, dsa_q_norm[li], dsa_k_norm[li])
        y_c = _dsa_main(q_t, qi_t, w_t, k_r, v_t, ki_b)
        y_d = _hgrn(p3, lbs[li], hg_norm[li])
        ys = (y_a.reshape(t, BW), y_b, y_c.reshape(t, BW), y_d.reshape(t, BW))
        xf = _merge(xf, h, ys, w_gate_b, w_branch_b, w_out_b, li)
        xf = _ffn(xf, ffn2_norm[li], *f2, li)
    return xf.reshape(b, l, D_MODEL)
```

```python
import functools
import math

import jax
import jax.numpy as jnp
from jax import lax
from jax.experimental import pallas as pl
from jax.experimental.pallas import tpu as pltpu

F32 = jnp.float32
BF16 = jnp.bfloat16
I32 = jnp.int32

D_MODEL = 1024
N_BRANCHES = 4
BW = D_MODEL // 2
D_FF = 2816
ROPE_THETA = 10000.0
NORM_EPS = 1e-6

M2_HEAD_DIM = 64
M2_HEADS = BW // M2_HEAD_DIM
M2_GROUPS = 2
M2_STATE = 64
M2_CONV = 4
M2_CHUNK = 128

S5_GROUP = 16
S5_GROUPS = BW // S5_GROUP
S5_STATE = 64
S5_CHUNK = 16
S5_GB = 8

DSA_HEAD_DIM = 64
DSA_HEADS = BW // DSA_HEAD_DIM
DSA_IDX_HEADS = 4
DSA_IDX_DIM = 64
DSA_TOPK = 256

HG_HEADS = 4
HG_KDIM = 128
HG_VDIM = BW // HG_HEADS
HG_CHUNK = 32

C_HGQ, C_HGF, C_HGI, C_HGG = 0, 512, 1024, 1536
C_DQ, C_DK, C_DV = 2048, 2560, 3072
C_MZ, C_MX = 3584, 4096
C_S5 = 4608
C_DQI = 5120
C_MBC = 5376
C_DKW = 5632
C_MDT = 5760
NP = 5888

LANE = 128
INT_MIN = -(2 ** 31)
NEG_BIG = -1e30
MASKED_SCORE = -3e38
DSA_TILE = 256
DSA_QK_AHEAD = 8
DSA_VPAD = 16
DSA_VROWS = DSA_HEADS * (DSA_HEAD_DIM + DSA_VPAD)
VMEM_LIMIT = 56 * 1024 * 1024


def _cparams(sem):
    return pltpu.CompilerParams(dimension_semantics=sem, vmem_limit_bytes=VMEM_LIMIT)


def _const_spec(shape):
    nd = len(shape)
    return pl.BlockSpec(shape, lambda *_: (0,) * nd, pipeline_mode=pl.Buffered(1))


def _layer_spec(shape, li):
    nd = len(shape)
    return pl.BlockSpec((None,) + shape, lambda *_: (li,) + (0,) * nd, pipeline_mode=pl.Buffered(1))


def _silu(x):
    return x * jax.nn.sigmoid(x)


def _ffn_body(x_ref, g_ref, w1_ref, w3_ref, w2_ref, o_ref, *, ff_chunk):
    x = x_ref[...]
    n = x * lax.rsqrt(jnp.mean(x * x, axis=-1, keepdims=True) + NORM_EPS) * g_ref[...]
    n = n.astype(BF16)
    acc = jnp.zeros(x.shape, F32)
    for c in range(D_FF // ff_chunk):
        sl = slice(c * ff_chunk, (c + 1) * ff_chunk)
        h1 = jnp.dot(n, w1_ref[:, sl], preferred_element_type=F32)
        h3 = jnp.dot(n, w3_ref[:, sl], preferred_element_type=F32)
        g = (_silu(h1) * h3).astype(BF16)
        acc = acc + jnp.dot(g, w2_ref[sl, :], preferred_element_type=F32)
    o_ref[...] = x + 0.5 * acc


def _ffn(x, gain, w1, w3, w2, li, tm=512, ff_chunk=256):
    t = x.shape[0]
    return pl.pallas_call(
        functools.partial(_ffn_body, ff_chunk=ff_chunk),
        grid=(t // tm,),
        in_specs=[
            pl.BlockSpec((tm, D_MODEL), lambda i: (i, 0)),
            _const_spec((1, D_MODEL)),
            _layer_spec((D_MODEL, D_FF), li),
            _layer_spec((D_MODEL, D_FF), li),
            _layer_spec((D_FF, D_MODEL), li),
        ],
        out_specs=pl.BlockSpec((tm, D_MODEL), lambda i: (i, 0)),
        out_shape=jax.ShapeDtypeStruct((t, D_MODEL), F32),
        compiler_params=_cparams(("parallel",)),
        name="ffn",
    )(x, gain.reshape(1, D_MODEL), w1, w3, w2)


def _proj_body(x_ref, g_ref, w_ref, p_ref, h_ref):
    x = x_ref[...]
    h = x * lax.rsqrt(jnp.mean(x * x, axis=-1, keepdims=True) + NORM_EPS) * g_ref[...]
    hb = h.astype(BF16)
    h_ref[...] = hb
    p_ref[...] = jnp.dot(hb, w_ref[...], preferred_element_type=F32)


def _proj(x, gain, w_pad, li, tm=256):
    t = x.shape[0]
    return pl.pallas_call(
        _proj_body,
        grid=(t // tm,),
        in_specs=[
            pl.BlockSpec((tm, D_MODEL), lambda i: (i, 0)),
            _const_spec((1, D_MODEL)),
            _layer_spec((D_MODEL, NP), li),
        ],
        out_specs=[
            pl.BlockSpec((tm, NP), lambda i: (i, 0)),
            pl.BlockSpec((tm, D_MODEL), lambda i: (i, 0)),
        ],
        out_shape=[
            jax.ShapeDtypeStruct((t, NP), F32),
            jax.ShapeDtypeStruct((t, D_MODEL), BF16),
        ],
        compiler_params=_cparams(("parallel",)),
        name="proj",
    )(x, gain.reshape(1, D_MODEL), w_pad)


def _mamba_body(z_ref, x_ref, bc_ref, dt_ref, cwx_ref, cbx_ref, cwbc_ref, cbbc_ref, dtb_ref, a_ref,
                dvec_ref, ng_ref, o_ref, xbuf, bcbuf, st_ref, ybuf, *, tt):
    q = M2_CHUNK
    pad = 8

    @pl.when(pl.program_id(1) == 0)
    def _():
        xbuf[0:pad, :] = jnp.zeros((pad, BW), F32)
        bcbuf[0:pad, :] = jnp.zeros((pad, 2 * LANE), F32)
        st_ref[...] = jnp.zeros(st_ref.shape, F32)

    xbuf[pad:pad + tt, :] = x_ref[...]
    bcbuf[pad:pad + tt, :] = bc_ref[...]

    def conv(buf, cw_ref, cb_ref):
        acc = cb_ref[...]
        for w in range(M2_CONV):
            off = pad - (M2_CONV - 1) + w
            acc = acc + cw_ref[w:w + 1, :] * buf[off:off + tt, :]
        return _silu(acc)

    xs = conv(xbuf, cwx_ref, cbx_ref)
    bc = conv(bcbuf, cwbc_ref, cbbc_ref)
    xbuf[0:pad, :] = xbuf[tt:tt + pad, :]
    bcbuf[0:pad, :] = bcbuf[tt:tt + pad, :]

    dtr = dt_ref[...] + dtb_ref[...]
    dt = jnp.maximum(dtr, 0.0) + jnp.log(1.0 + jnp.exp(-jnp.abs(dtr)))
    ad = dt * a_ref[...]

    rq = lax.broadcasted_iota(I32, (q, q), 0)
    cq = lax.broadcasted_iota(I32, (q, q), 1)
    causal = rq >= cq
    tril = causal.astype(F32)

    for c in range(tt // q):
        rows = slice(c * q, (c + 1) * q)
        a_cs = jnp.dot(tril, ad[rows], precision=lax.Precision.HIGHEST, preferred_element_type=F32)
        a_cs_t = a_cs.T
        xs_c = xs[rows]
        dt_c = dt[rows]
        for g in range(M2_GROUPS):
            b_g = bc[rows, g * M2_STATE:(g + 1) * M2_STATE]
            c_g = bc[rows, LANE + g * M2_STATE:LANE + (g + 1) * M2_STATE]
            b_gt = b_g.T
            c_gb = c_g.astype(BF16)
            gmat = jnp.dot(c_gb, b_gt.astype(BF16), preferred_element_type=F32)
            for hh in range(M2_HEADS // M2_GROUPS):
                h = g * (M2_HEADS // M2_GROUPS) + hh
                col = a_cs[:, h:h + 1]
                row = a_cs_t[h:h + 1, :]
                a_last = a_cs[q - 1:q, h:h + 1]
                decay = jnp.exp(jnp.where(causal, col - row, -jnp.inf))
                xdt = (xs_c[:, h * M2_HEAD_DIM:(h + 1) * M2_HEAD_DIM] * dt_c[:, h:h + 1]).astype(BF16)
                y_diag = jnp.dot((gmat * decay).astype(BF16), xdt, preferred_element_type=F32)
                st_in = st_ref[h]
                y_off = jnp.dot(c_gb, st_in.astype(BF16), preferred_element_type=F32) * jnp.exp(col)
                bw_t = (b_gt * jnp.exp(a_last - row)).astype(BF16)
                st_ref[h] = jnp.exp(a_last) * st_in + jnp.dot(bw_t, xdt, preferred_element_type=F32)
                ybuf[rows, h * M2_HEAD_DIM:(h + 1) * M2_HEAD_DIM] = y_diag + y_off

    y = ybuf[...] + dvec_ref[...] * xs
    yz = y * _silu(z_ref[...])
    o_ref[...] = (yz * lax.rsqrt(jnp.mean(yz * yz, axis=-1, keepdims=True) + NORM_EPS) * ng_ref[...]).astype(BF16)


def _mamba(p3, conv_w, conv_b, dt_bias, a_log, d_skip, norm_g, tt=256):
    b, l, _ = p3.shape
    cwx, cwbc = conv_w[:, :BW], conv_w[:, BW:]
    cbx, cbbc = conv_b[:BW].reshape(1, BW), conv_b[BW:].reshape(1, 2 * LANE)
    dtb = jnp.zeros((1, LANE), F32).at[0, :M2_HEADS].set(dt_bias)
    a = jnp.zeros((1, LANE), F32).at[0, :M2_HEADS].set(-jnp.exp(a_log))
    dvec = jnp.repeat(d_skip, M2_HEAD_DIM).reshape(1, BW)

    def col(width, start):
        return pl.BlockSpec((None, tt, width), lambda bi, ti: (bi, ti, start // width))

    return pl.pallas_call(
        functools.partial(_mamba_body, tt=tt),
        grid=(b, l // tt),
        in_specs=[
            col(BW, C_MZ), col(BW, C_MX), col(2 * LANE, C_MBC), col(LANE, C_MDT),
            _const_spec((M2_CONV, BW)), _const_spec((1, BW)),
            _const_spec((M2_CONV, 2 * LANE)), _const_spec((1, 2 * LANE)),
            _const_spec((1, LANE)), _const_spec((1, LANE)),
            _const_spec((1, BW)), _const_spec((1, BW)),
        ],
        out_specs=pl.BlockSpec((None, tt, BW), lambda bi, ti: (bi, ti, 0)),
        out_shape=jax.ShapeDtypeStruct((b, l, BW), BF16),
        scratch_shapes=[
            pltpu.VMEM((tt + 8, BW), F32),
            pltpu.VMEM((tt + 8, 2 * LANE), F32),
            pltpu.VMEM((M2_HEADS, M2_STATE, M2_HEAD_DIM), F32),
            pltpu.VMEM((tt, BW), F32),
        ],
        compiler_params=_cparams(("parallel", "arbitrary")),
        name="mamba",
    )(p3, p3, p3, p3, cwx, cbx, cwbc, cbbc, dtb, a, dvec, norm_g.reshape(1, BW))


def _s5_tables(lam_re, lam_im, b_re, b_im, c_re, c_im, log_dt):
    ck, gs, ps = S5_CHUNK, S5_GROUP, S5_STATE
    step = jnp.exp(log_dt)[:, None]
    ldt_re, ldt_im = lam_re * step, lam_im * step
    mag = jnp.exp(ldt_re)
    ab_re, ab_im = mag * jnp.cos(ldt_im), mag * jnp.sin(ldt_im)
    den = lam_re * lam_re + lam_im * lam_im
    q_re = ((ab_re - 1.0) * lam_re + ab_im * lam_im) / den
    q_im = (ab_im * lam_re - (ab_re - 1.0) * lam_im) / den
    bb_re = q_re[..., None] * b_re - q_im[..., None] * b_im
    bb_im = q_re[..., None] * b_im + q_im[..., None] * b_re
    n = jnp.arange(ck + 1, dtype=F32)[:, None, None]
    pm = jnp.exp(n * ldt_re[None])
    pw_re, pw_im = pm * jnp.cos(n * ldt_im[None]), pm * jnp.sin(n * ldt_im[None])
    ab_r = pw_re[..., None] * bb_re[None] - pw_im[..., None] * bb_im[None]
    ab_i = pw_re[..., None] * bb_im[None] + pw_im[..., None] * bb_re[None]
    kk = jnp.einsum("gip,ngpj->ngij", c_re, ab_r) - jnp.einsum("gip,ngpj->ngij", c_im, ab_i)
    tq = jnp.arange(ck)
    lag = tq[None, :] - tq[:, None]
    kl = jnp.where((lag >= 0)[:, :, None, None, None], kk[jnp.clip(lag, 0, ck)], 0.0)
    m = kl.transpose(2, 0, 4, 1, 3).reshape(S5_GROUPS, ck * gs, ck * gs)
    rev = ck - 1 - tq
    w_re = ab_r[rev].transpose(1, 0, 3, 2).reshape(S5_GROUPS, ck * gs, ps)
    w_im = ab_i[rev].transpose(1, 0, 3, 2).reshape(S5_GROUPS, ck * gs, ps)
    ca_r = c_re[None] * pw_re[1:, :, None, :] - c_im[None] * pw_im[1:, :, None, :]
    ca_i = c_re[None] * pw_im[1:, :, None, :] + c_im[None] * pw_re[1:, :, None, :]
    v_re = ca_r.transpose(1, 3, 0, 2).reshape(S5_GROUPS, ps, ck * gs)
    v_im = (-ca_i).transpose(1, 3, 0, 2).reshape(S5_GROUPS, ps, ck * gs)
    a16_re = pw_re[ck].reshape(S5_GROUPS // S5_GB, 1, S5_GB * ps)
    a16_im = pw_im[ck].reshape(S5_GROUPS // S5_GB, 1, S5_GB * ps)
    return (m.astype(BF16), w_re.astype(BF16), w_im.astype(BF16), v_re.astype(BF16), v_im.astype(BF16),
            a16_re, a16_im)


def _s5_perm():
    ck, gb, gs = S5_CHUNK, S5_GB, S5_GROUP
    r = jnp.arange(ck * gb * gs)
    dest = ((r % (gb * gs)) // gs) * (ck * gs) + (r // (gb * gs)) * gs + r % gs
    return (dest[:, None] == r[None, :]).astype(BF16)


def _s5_body(u_ref, perm_ref, m_ref, wre_ref, wim_ref, vre_ref, vim_ref, are_ref, aim_ref, y_ref,
             xre, xim, sre, sim, ybuf, *, lc):
    p, rw, bw = S5_STATE, S5_CHUNK * S5_GROUP, S5_GB * S5_GROUP
    u = jnp.concatenate([u_ref[pl.ds(jt, lc, stride=S5_CHUNK), :] for jt in range(S5_CHUNK)], axis=1).astype(BF16)
    ug = jnp.dot(u, perm_ref[...], preferred_element_type=F32).astype(BF16)
    for g in range(S5_GB):
        ugg = ug[:, g * rw:(g + 1) * rw]
        xre[:, g * p:(g + 1) * p] = jnp.dot(ugg, wre_ref[g], preferred_element_type=F32)
        xim[:, g * p:(g + 1) * p] = jnp.dot(ugg, wim_ref[g], preferred_element_type=F32)
    a_re = are_ref[...]
    a_im = aim_ref[...]

    def step(c, carry):
        s_re, s_im = carry
        sre[pl.ds(c, 1), :] = s_re
        sim[pl.ds(c, 1), :] = s_im
        n_re = a_re * s_re - a_im * s_im + xre[pl.ds(c, 1), :]
        n_im = a_re * s_im + a_im * s_re + xim[pl.ds(c, 1), :]
        return n_re, n_im

    zero = jnp.zeros((1, S5_GB * p), F32)
    lax.fori_loop(0, lc, step, (zero, zero))
    for g in range(S5_GB):
        y = jnp.dot(ug[:, g * rw:(g + 1) * rw], m_ref[g], preferred_element_type=F32)
        y = y + jnp.dot(sre[:, g * p:(g + 1) * p].astype(BF16), vre_ref[g], preferred_element_type=F32)
        y = y + jnp.dot(sim[:, g * p:(g + 1) * p].astype(BF16), vim_ref[g], preferred_element_type=F32)
        ybuf[:, g * rw:(g + 1) * rw] = y.astype(BF16)
    y_rows = lax.dot_general(ybuf[...], perm_ref[...], (((1,), (1,)), ((), ())), preferred_element_type=F32)
    for jt in range(S5_CHUNK):
        y_ref[pl.ds(jt, lc, stride=S5_CHUNK), :] = y_rows[:, jt * bw:(jt + 1) * bw]


def _s5_core(p3, perm, tables, li):
    b, l, _ = p3.shape
    m, w_re, w_im, v_re, v_im, a_re, a_im = tables
    gb, sw, gw, bw = S5_GB, S5_GB * S5_STATE, S5_CHUNK * S5_GROUP, S5_GB * S5_GROUP
    nb, lc, rw = S5_GROUPS // S5_GB, l // S5_CHUNK, S5_CHUNK * S5_GB * S5_GROUP

    def grp(shape):
        return pl.BlockSpec((None, gb) + shape, lambda bi, gi: (li, gi, 0, 0))

    def blk(shape):
        return pl.BlockSpec((None, None) + shape, lambda bi, gi: (li, gi, 0, 0))

    return pl.pallas_call(
        functools.partial(_s5_body, lc=lc),
        grid=(b, nb),
        in_specs=[
            pl.BlockSpec((None, l, bw), lambda bi, gi: (bi, 0, C_S5 // bw + gi)),
            _const_spec((rw, rw)),
            grp((gw, gw)), grp((gw, S5_STATE)), grp((gw, S5_STATE)), grp((S5_STATE, gw)), grp((S5_STATE, gw)),
            blk((1, sw)), blk((1, sw)),
        ],
        out_specs=pl.BlockSpec((None, l, bw), lambda bi, gi: (bi, 0, gi)),
        out_shape=jax.ShapeDtypeStruct((b, l, BW), F32),
        scratch_shapes=[pltpu.VMEM((lc, sw), F32) for _ in range(4)] + [pltpu.VMEM((lc, rw), BF16)],
        compiler_params=_cparams(("parallel", "parallel")),
        name="s5_core",
    )(p3, perm, m, w_re, w_im, v_re, v_im, a_re, a_im)


def _s5_post_body(y_ref, u_ref, d_ref, w_ref, o_ref):
    y = y_ref[...] + d_ref[...] * u_ref[...]
    c = math.sqrt(2.0 / math.pi)
    ge = 0.5 * y * (1.0 + jnp.tanh(c * (y + 0.044715 * (y * y * y))))
    g2 = jnp.dot(ge.astype(BF16), w_ref[...], preferred_element_type=F32)
    o_ref[...] = (g2[:, :BW] * jax.nn.sigmoid(g2[:, BW:])).astype(BF16)


def _s5_post(y_ssm, p, d_skip, w_glu, tm=512):
    t = y_ssm.shape[0]
    return pl.pallas_call(
        _s5_post_body,
        grid=(t // tm,),
        in_specs=[
            pl.BlockSpec((tm, BW), lambda i: (i, 0)),
            pl.BlockSpec((tm, BW), lambda i: (i, C_S5 // BW)),
            _const_spec((1, BW)),
            _const_spec((BW, 2 * BW)),
        ],
        out_specs=pl.BlockSpec((tm, BW), lambda i: (i, 0)),
        out_shape=jax.ShapeDtypeStruct((t, BW), BF16),
        compiler_params=_cparams(("parallel",)),
        name="s5_post",
    )(y_ssm, p, d_skip.reshape(1, BW), w_glu)


def _s5(p, b, l, tables, li, d_skip, w_glu):
    y_ssm = _s5_core(p.reshape(b, l, NP), _s5_perm(), tables, li).reshape(b * l, BW)
    return _s5_post(y_ssm, p, d_skip, w_glu)


def _dsa_prep_body(q_ref, k_ref, v_ref, qi_ref, kw_ref, pos_ref, qn_ref, kn_ref, freq_ref, sgn_ref, ones_ref,
                   qt_ref, ko_ref, vt_ref, kio_ref, qit_ref, wt_ref):
    ang = _round_to_f32_grid(pos_ref[...].astype(F32) * freq_ref[...])
    reps = BW // LANE
    cos = jnp.concatenate([jnp.cos(ang)] * reps, axis=1)
    sin = jnp.concatenate([jnp.sin(ang) * sgn_ref[...]] * reps, axis=1)
    lane = lax.broadcasted_iota(I32, cos.shape, 1)
    first_half = (lane % DSA_HEAD_DIM) < (DSA_HEAD_DIM // 2)

    def norm_rope(x, gain, scale):
        x2 = x * x
        hi = x2.astype(BF16)
        lo = (x2 - hi.astype(F32)).astype(BF16)
        ms = (jnp.dot(hi, ones_ref[...], preferred_element_type=F32)
              + jnp.dot(lo, ones_ref[...], preferred_element_type=F32))
        xn = x * lax.rsqrt(ms * (1.0 / DSA_HEAD_DIM) + NORM_EPS) * gain
        half = DSA_HEAD_DIM // 2
        partner = jnp.where(first_half, pltpu.roll(xn, BW - half, 1), pltpu.roll(xn, half, 1))
        return (xn * cos + partner * sin) * scale

    qt_ref[...] = norm_rope(q_ref[...], qn_ref[...], DSA_HEAD_DIM ** -0.5 * math.log2(math.e)).T.astype(BF16)
    ko_ref[...] = norm_rope(k_ref[...], kn_ref[...], 1.0).astype(BF16)
    vt = v_ref[...].T
    ones = jnp.ones((DSA_VPAD, vt.shape[1]), F32)
    pieces = []
    for h in range(DSA_HEADS):
        pieces += [vt[h * DSA_HEAD_DIM:(h + 1) * DSA_HEAD_DIM, :], ones]
    vt_ref[...] = jnp.concatenate(pieces, axis=0).astype(BF16)
    kw = kw_ref[...]
    kio_ref[...] = kw.astype(BF16)
    qit_ref[...] = qi_ref[...].T.astype(BF16)
    wt = kw.T[DSA_IDX_DIM:DSA_IDX_DIM + 8, :]
    wt_ref[...] = wt * (DSA_IDX_HEADS ** -0.5 * DSA_IDX_DIM ** -0.5)


def _dsa_prep(p3, pos, q_norm, k_norm):
    b, l, _ = p3.shape
    tt = DSA_TILE
    half = DSA_HEAD_DIM // 2
    inv_freq = _round_to_f32_grid(ROPE_THETA ** (-jnp.arange(half, dtype=F32) / half))
    freq = jnp.tile(jnp.concatenate([inv_freq, inv_freq]), LANE // DSA_HEAD_DIM).reshape(1, LANE)
    sgn = jnp.tile(jnp.concatenate([-jnp.ones(half, F32), jnp.ones(half, F32)]), LANE // DSA_HEAD_DIM).reshape(1, LANE)
    hid = jnp.arange(BW) // DSA_HEAD_DIM
    ones_bd = (hid[:, None] == hid[None, :]).astype(BF16)
    qn = jnp.tile(q_norm, DSA_HEADS).reshape(1, BW)
    kn = jnp.tile(k_norm, DSA_HEADS).reshape(1, BW)

    def col(width, start):
        return pl.BlockSpec((None, tt, width), lambda bi, ti: (bi, ti, start // width))

    def rows_t(n):
        return pl.BlockSpec((None, n, tt), lambda bi, ti: (bi, 0, ti))

    return pl.pallas_call(
        _dsa_prep_body,
        grid=(b, l // tt),
        in_specs=[
            col(BW, C_DQ), col(BW, C_DK), col(BW, C_DV), col(2 * LANE, C_DQI), col(LANE, C_DKW),
            pl.BlockSpec((None, tt, 1), lambda bi, ti: (bi, ti, 0)),
            _const_spec((1, BW)), _const_spec((1, BW)), _const_spec((1, LANE)), _const_spec((1, LANE)),
            _const_spec((BW, BW)),
        ],
        out_specs=[
            rows_t(BW),
            pl.BlockSpec((None, tt, BW), lambda bi, ti: (bi, ti, 0)),
            pl.BlockSpec((None, None, DSA_VROWS, tt), lambda bi, ti: (bi, ti, 0, 0)),
            pl.BlockSpec((None, tt, LANE), lambda bi, ti: (bi, ti, 0)),
            rows_t(2 * LANE),
            rows_t(8),
        ],
        out_shape=[
            jax.ShapeDtypeStruct((b, BW, l), BF16),
            jax.ShapeDtypeStruct((b, l, BW), BF16),
            jax.ShapeDtypeStruct((b, l // tt, DSA_VROWS, tt), BF16),
            jax.ShapeDtypeStruct((b, l, LANE), BF16),
            jax.ShapeDtypeStruct((b, 2 * LANE, l), BF16),
            jax.ShapeDtypeStruct((b, 8, l), F32),
        ],
        compiler_params=_cparams(("parallel", "parallel")),
        name="dsa_prep",
    )(p3, p3, p3, p3, p3, pos.reshape(b, l, 1), qn, kn, freq, sgn, ones_bd)


def _fold8(w, op):
    parts = [w[8 * k:8 * (k + 1), :] for k in range(w.shape[0] // 8)]
    while len(parts) > 1:
        parts = [op(parts[2 * k], parts[2 * k + 1]) for k in range(len(parts) // 2)]
    return parts[0]


def _pattern_to_float(u):
    bits = jnp.where(u < 0, u ^ INT_MIN, ~u)
    return lax.bitcast_convert_type(bits, F32)


def _round_to_f32_grid(x):
    ef = lax.bitcast_convert_type(x, I32) & 0x7F800000
    up = lax.bitcast_convert_type((150 << 23) - (ef - (127 << 23)), F32)
    down = lax.bitcast_convert_type(ef - (23 << 23), F32)
    y = x * up
    f = jnp.floor(y)
    r = f + jnp.where(y - f >= 0.5, 1.0, 0.0)
    scalable = (ef >= (24 << 23)) & (ef <= (253 << 23))
    return jnp.where(scalable, r * down, x)


def _dsa_body(qt_ref, qit_ref, wt_ref, k_ref, vt_ref, ki_ref, o_ref, sc_ref, scb_ref, m_ref, acc_ref):
    t = DSA_TILE
    d = DSA_HEAD_DIM
    i = pl.program_id(1)
    key_l = lax.broadcasted_iota(I32, (t, t), 0)
    qry_g = i * t + lax.broadcasted_iota(I32, (t, t), 1)

    def index_tile(jt, masked):
        ks = pl.ds(pl.multiple_of(jt * t, t), t)
        kib = ki_ref[ks, 0:DSA_IDX_DIM]
        sc = jnp.zeros((t, t), F32)
        for h in range(DSA_IDX_HEADS):
            dd = jnp.dot(kib, qit_ref[h * DSA_IDX_DIM:(h + 1) * DSA_IDX_DIM, :], preferred_element_type=F32)
            sc = sc + jnp.maximum(dd, 0.0) * wt_ref[h:h + 1, :]
        if masked:
            sc = jnp.where(jt * t + key_l <= qry_g, sc, MASKED_SCORE)
        sc_ref[ks, :] = sc
        scb_ref[ks, :] = sc.astype(BF16)

    def index_loop(jt, carry):
        index_tile(jt, False)
        return carry

    lax.fori_loop(0, i, index_loop, 0)
    index_tile(i, True)

    n_tiles = i + 1

    def sweep(fn, init):
        return lax.fori_loop(0, n_tiles, lambda j, a: fn(pl.multiple_of(j * t, t), t, a), init)

    def count(pred):
        def body(r0, n, cnt):
            x = sc_ref[pl.ds(r0, n), :]
            return cnt + _fold8(jnp.where(pred(x, r0 + key_l), 1.0, 0.0), jnp.add)
        return jnp.sum(sweep(body, jnp.zeros((8, t), F32)), axis=0, keepdims=True)

    def count_coarse(mid):
        def body(r0, n, cnt):
            w = jnp.where(scb_ref[pl.ds(r0, n), :] >= mid, jnp.ones((), BF16), jnp.zeros((), BF16))
            parts = [w[16 * k:16 * (k + 1), :] for k in range(n // 16)]
            while len(parts) > 1:
                parts = [parts[2 * k] + parts[2 * k + 1] for k in range(len(parts) // 2)]
            return cnt + parts[0].astype(F32)
        return jnp.sum(sweep(body, jnp.zeros((16, t), F32)), axis=0, keepdims=True)

    def coarse_pass(b, v16):
        cand = v16 | jnp.left_shift(jnp.int32(1), 15 - b)
        mid = _pattern_to_float(jnp.left_shift(cand, 16)).astype(BF16)
        return jnp.where(count_coarse(mid) >= DSA_TOPK, cand, v16)

    v16 = lax.fori_loop(0, 16, coarse_pass, jnp.zeros((1, t), I32))

    base = jnp.left_shift(v16, 16) - (1 << 15)

    def fine_pass(b, off):
        cand = off | jnp.left_shift(jnp.int32(1), 16 - b)
        mid = _pattern_to_float(base + cand)
        tot = count(lambda x, key_g: x >= mid)
        return jnp.where(tot >= DSA_TOPK, cand, off)

    off = lax.fori_loop(0, 17, fine_pass, jnp.zeros((1, t), I32))
    floor = _pattern_to_float(base + off)

    def nearest(pred, fill, op):
        def body(r0, n, acc):
            x = sc_ref[pl.ds(r0, n), :]
            return op(acc, _fold8(jnp.where(pred(x), x, fill), op))
        red = jnp.min if op is jnp.minimum else jnp.max
        return red(sweep(body, jnp.full((8, t), fill, F32)), axis=0, keepdims=True)

    def ranks(th):
        def body(r0, n, cnt):
            x = sc_ref[pl.ds(r0, n), :]
            return (cnt[0] + _fold8(jnp.where(x > th, 1.0, 0.0), jnp.add),
                    cnt[1] + _fold8(jnp.where(x >= th, 1.0, 0.0), jnp.add))
        gt, ge = sweep(body, (jnp.zeros((8, t), F32), jnp.zeros((8, t), F32)))
        return jnp.sum(gt, axis=0, keepdims=True), jnp.sum(ge, axis=0, keepdims=True)

    thr = nearest(lambda x: x >= floor, jnp.inf, jnp.minimum)
    n_gt, n_ge = ranks(thr)

    def off_rank(carry):
        _, gt, ge = carry
        return jnp.max(jnp.where((gt >= DSA_TOPK) | (ge < DSA_TOPK), 1, 0)) > 0

    def step_thr(carry):
        th, gt, ge = carry
        up = nearest(lambda x: x > th, jnp.inf, jnp.minimum)
        down = nearest(lambda x: x < th, -jnp.inf, jnp.maximum)
        th = jnp.where(gt >= DSA_TOPK, up, jnp.where(ge < DSA_TOPK, down, th))
        return (th,) + ranks(th)

    thr, n_gt, n_ge = lax.while_loop(off_rank, step_thr, (thr, n_gt, n_ge))
    need = jnp.where(thr == MASKED_SCORE, 0.0, DSA_TOPK - n_gt)

    dv = d + DSA_VPAD
    m_ref[...] = jnp.full(m_ref.shape, NEG_BIG, F32)
    acc_ref[...] = jnp.zeros(acc_ref.shape, F32)
    incl = (lax.broadcasted_iota(I32, (t, t), 1) <= key_l).astype(BF16)

    def select_bias(jt, ties_before):
        x = sc_ref[pl.ds(pl.multiple_of(jt * t, t), t), :]
        tie = x == thr
        rank = ties_before + jnp.dot(incl, jnp.where(tie, 1.0, 0.0).astype(BF16), preferred_element_type=F32)
        sel = (x > thr) | (tie & (rank <= need))
        return jnp.where(sel, 0.0, NEG_BIG), rank[t - 1:t, :]

    def attend_tile(jt, bias):
        ks = pl.ds(pl.multiple_of(jt * t, t), t)

        def logits(h):
            return jnp.dot(k_ref[ks, h * d:(h + 1) * d], qt_ref[h * d:(h + 1) * d, :], preferred_element_type=F32) + bias

        lgs = {h: logits(h) for h in range(DSA_QK_AHEAD)}
        for h in range(DSA_HEADS):
            if h + DSA_QK_AHEAD < DSA_HEADS:
                lgs[h + DSA_QK_AHEAD] = logits(h + DSA_QK_AHEAD)
            hv = slice(h * dv, (h + 1) * dv)
            m_old = m_ref[h:h + 1, :]
            m_new = jnp.maximum(m_old, jnp.max(_fold8(lgs[h], jnp.maximum), axis=0, keepdims=True))
            alpha = jnp.exp2(m_old - m_new)
            pexp = jnp.exp2(lgs[h] - m_new).astype(BF16)
            m_ref[h:h + 1, :] = m_new
            acc_ref[hv, :] = alpha * acc_ref[hv, :] + jnp.dot(vt_ref[jt, hv, :], pexp, preferred_element_type=F32)

    def attend_loop(jt, ties):
        bias, ties = select_bias(jt, ties)
        attend_tile(jt, bias)
        return ties

    lax.fori_loop(0, i + 1, attend_loop, jnp.zeros((1, t), F32))
    outs = [acc_ref[h * dv:h * dv + d, :] / acc_ref[h * dv + d:h * dv + d + 1, :] for h in range(DSA_HEADS)]
    o_ref[...] = jnp.concatenate(outs, axis=0).T.astype(BF16)


def _dsa_main(q_t, qi_t, w_t, k_r, v_t, ki_b):
    b, l, _ = k_r.shape
    t = DSA_TILE

    def rows_t(n):
        return pl.BlockSpec((None, n, t), lambda bi, qi: (bi, 0, qi))

    def whole(shape):
        nd = len(shape)
        return pl.BlockSpec((None,) + shape, lambda bi, qi: (bi,) + (0,) * nd, pipeline_mode=pl.Buffered(1))

    return pl.pallas_call(
        _dsa_body,
        grid=(b, l // t),
        in_specs=[rows_t(BW), rows_t(2 * LANE), rows_t(8),
                  whole((l, BW)), whole((l // t, DSA_VROWS, t)), whole((l, LANE))],
        out_specs=pl.BlockSpec((None, t, BW), lambda bi, qi: (bi, qi, 0)),
        out_shape=jax.ShapeDtypeStruct((b, l, BW), BF16),
        scratch_shapes=[
            pltpu.VMEM((l, t), F32),
            pltpu.VMEM((l, t), BF16),
            pltpu.VMEM((DSA_HEADS, t), F32),
            pltpu.VMEM((DSA_VROWS, t), F32),
        ],
        compiler_params=_cparams(("parallel", "arbitrary")),
        name="dsa_main",
    )(q_t, qi_t, w_t, k_r, v_t, ki_b)


def _hgrn_body(q_ref, f_ref, i_ref, g_ref, lb_ref, ng_ref, o_ref, st_ref, obuf, *, tt):
    ck = HG_CHUNK

    @pl.when(pl.program_id(1) == 0)
    def _():
        st_ref[...] = jnp.zeros(st_ref.shape, F32)

    lb = lb_ref[...]
    f = f_ref[...]
    qq = _silu(q_ref[...])
    log_f = jnp.log(lb + (1.0 - lb) * jax.nn.sigmoid(f))
    key = (1.0 - lb) * jax.nn.sigmoid(-f)
    v = i_ref[...]

    r = lax.broadcasted_iota(I32, (tt, tt), 0)
    c = lax.broadcasted_iota(I32, (tt, tt), 1)
    blk_causal = ((r // ck) == (c // ck)) & (r >= c)
    cum = jnp.dot(blk_causal.astype(F32), log_f, precision=lax.Precision.HIGHEST, preferred_element_type=F32)
    q_dec = qq * jnp.exp(cum)
    k_dec = key * jnp.exp(-cum)
    q_dec_b = q_dec.astype(BF16)
    k_dec_b = k_dec.astype(BF16)
    v_b = v.astype(BF16)

    for h in range(HG_HEADS):
        hs = slice(h * HG_KDIM, (h + 1) * HG_KDIM)
        att = lax.dot_general(q_dec_b[:, hs], k_dec_b[:, hs], (((1,), (1,)), ((), ())), preferred_element_type=F32)
        att = jnp.where(blk_causal, att, 0.0).astype(BF16)
        obuf[:, hs] = jnp.dot(att, v_b[:, hs], preferred_element_type=F32)

    for cc in range(tt // ck):
        rows = slice(cc * ck, (cc + 1) * ck)
        last = cum[cc * ck + ck - 1:cc * ck + ck, :]
        k_end = (key[rows] * jnp.exp(last - cum[rows])).astype(BF16)
        dec = jnp.exp(last)
        for h in range(HG_HEADS):
            hs = slice(h * HG_KDIM, (h + 1) * HG_KDIM)
            st = st_ref[h]
            o_inter = lax.dot_general(q_dec_b[rows, hs], st.astype(BF16), (((1,), (1,)), ((), ())),
                                      preferred_element_type=F32)
            obuf[rows, hs] = obuf[rows, hs] + o_inter
            upd = lax.dot_general(v_b[rows, hs], k_end[:, hs], (((0,), (0,)), ((), ())),
                                  preferred_element_type=F32)
            st_ref[h] = st * dec[:, hs] + upd

    gate = jax.nn.sigmoid(g_ref[...])
    for h in range(HG_HEADS):
        hs = slice(h * HG_VDIM, (h + 1) * HG_VDIM)
        o = obuf[:, hs]
        on = o * lax.rsqrt(jnp.mean(o * o, axis=-1, keepdims=True) + NORM_EPS) * ng_ref[:, hs]
        o_ref[:, hs] = (on * gate[:, hs]).astype(BF16)


def _hgrn(p3, lb, norm_g, tt=256):
    b, l, _ = p3.shape

    def col(start):
        return pl.BlockSpec((None, tt, BW), lambda bi, ti: (bi, ti, start // BW))

    return pl.pallas_call(
        functools.partial(_hgrn_body, tt=tt),
        grid=(b, l // tt),
        in_specs=[col(C_HGQ), col(C_HGF), col(C_HGI), col(C_HGG), _const_spec((1, BW)), _const_spec((1, BW))],
        out_specs=pl.BlockSpec((None, tt, BW), lambda bi, ti: (bi, ti, 0)),
        out_shape=jax.ShapeDtypeStruct((b, l, BW), BF16),
        scratch_shapes=[pltpu.VMEM((HG_HEADS, HG_VDIM, HG_KDIM), F32), pltpu.VMEM((tt, BW), F32)],
        compiler_params=_cparams(("parallel", "arbitrary")),
        name="hgrn",
    )(p3, p3, p3, p3, lb.reshape(1, BW), norm_g.reshape(1, BW))


def _merge_body(x_ref, h_ref, ya_ref, yb_ref, yc_ref, yd_ref, wg_ref, wb_ref, wo_ref, o_ref):
    h = h_ref[...]
    merged = jnp.zeros(x_ref.shape, F32)
    for g, y_ref in enumerate((ya_ref, yb_ref, yc_ref, yd_ref)):
        gate = jax.nn.sigmoid(jnp.dot(h, wg_ref[:, g * D_MODEL:(g + 1) * D_MODEL], preferred_element_type=F32))
        merged = merged + gate * jnp.dot(y_ref[...], wb_ref[g], preferred_element_type=F32)
    o_ref[...] = x_ref[...] + jnp.dot(merged.astype(BF16), wo_ref[...], preferred_element_type=F32)


def _merge(x, h, ys, w_gate, w_branch, w_out, li, tm=512):
    t = x.shape[0]
    row = lambda w: pl.BlockSpec((tm, w), lambda i: (i, 0))
    return pl.pallas_call(
        _merge_body,
        grid=(t // tm,),
        in_specs=[row(D_MODEL), row(D_MODEL), row(BW), row(BW), row(BW), row(BW),
                  _layer_spec((D_MODEL, N_BRANCHES * D_MODEL), li),
                  _layer_spec((N_BRANCHES, BW, D_MODEL), li),
                  _layer_spec((D_MODEL, D_MODEL), li)],
        out_specs=row(D_MODEL),
        out_shape=jax.ShapeDtypeStruct((t, D_MODEL), F32),
        compiler_params=_cparams(("parallel",)),
        name="merge",
    )(x, h, *ys, w_gate, w_branch, w_out)


def _pad_w_in(w_in):
    o = 0
    seg = {}
    for name, width in (("mz", BW), ("mx", BW), ("mbc", 2 * LANE), ("mdt", M2_HEADS), ("s5", BW),
                        ("dq", BW), ("dk", BW), ("dv", BW), ("dqi", 2 * LANE), ("dki", DSA_IDX_DIM),
                        ("dwi", DSA_IDX_HEADS), ("hq", BW), ("hf", BW), ("hi", BW), ("hg", BW)):
        seg[name] = w_in[:, o:o + width]
        o += width
    z = lambda n: jnp.zeros((D_MODEL, n), w_in.dtype)
    cols = [seg["hq"], seg["hf"], seg["hi"], seg["hg"], seg["dq"], seg["dk"], seg["dv"], seg["mz"], seg["mx"],
            seg["s5"], seg["dqi"], seg["mbc"],
            seg["dki"], seg["dwi"], z(LANE - DSA_IDX_DIM - DSA_IDX_HEADS),
            seg["mdt"], z(LANE - M2_HEADS)]
    return jnp.concatenate(cols, axis=1)


def kernel(x, positions, ffn1_norm, ffn1_w1, ffn1_w3, ffn1_w2, mix_norm, w_in, w_gate, w_branch, w_out, m2_conv_w, m2_conv_b, m2_dt_bias, m2_a_log, m2_d, m2_norm, s5_lam_re, s5_lam_im, s5_b_re, s5_b_im, s5_c_re, s5_c_im, s5_d, s5_log_dt, s5_w_glu, dsa_q_norm, dsa_k_norm, hg_gamma, hg_norm, ffn2_norm, ffn2_w1, ffn2_w3, ffn2_w2):
    b, l, _ = x.shape
    t = b * l
    depth = w_in.shape[0]
    lbs = jnp.cumsum(jax.nn.softmax(hg_gamma.astype(F32), axis=0), axis=0)
    lbs = lbs - lbs[0]
    xf = x.reshape(t, D_MODEL)
    w_pad = jax.vmap(_pad_w_in)(w_in).astype(BF16)
    s5_tabs = jax.vmap(_s5_tables)(s5_lam_re, s5_lam_im, s5_b_re, s5_b_im, s5_c_re, s5_c_im, s5_log_dt)
    f1 = [w.astype(BF16) for w in (ffn1_w1, ffn1_w3, ffn1_w2)]
    f2 = [w.astype(BF16) for w in (ffn2_w1, ffn2_w3, ffn2_w2)]
    w_gate_b, w_branch_b, w_out_b = w_gate.astype(BF16), w_branch.astype(BF16), w_out.astype(BF16)
    for li in range(depth):
        xf = _ffn(xf, ffn1_norm[li], *f1, li)
        p, h = _proj(xf, mix_norm[li], w_pad, li)
        p3 = p.reshape(b, l, NP)
        y_a = _mamba(p3, m2_conv_w[li], m2_conv_b[li], m2_dt_bias[li], m2_a_log[li], m2_d[li], m2_norm[li])
        y_b = _s5(p, b, l, s5_tabs, li, s5_d[li], s5_w_glu[li].astype(BF16))
        q_t, k_r, v_t, ki_b, qi_t, w_t = _dsa_prep(p3, positions, dsa_q_norm[li], dsa_k_norm[li])
        y_c = _dsa_main(q_t, qi_t, w_t, k_r, v_t, ki_b)
        y_d = _hgrn(p3, lbs[li], hg_norm[li])
        ys = (y_a.reshape(t, BW), y_b, y_c.reshape(t, BW), y_d.reshape(t, BW))
        xf = _merge(xf, h, ys, w_gate_b, w_branch_b, w_out_b, li)
        xf = _ffn(xf, ffn2_norm[li], *f2, li)
    return xf.reshape(b, l, D_MODEL)
```
